```python
import math
import jax, jax.numpy as jnp
from jax import lax
import numpy as np

D_MODEL = 1024
BATCH = 4
SEQ = 4096
DEPTH = 1

N_MEM = 256
SB_HEADS = 8
SB_HEAD_DIM = 64
SB_WIDTH = SB_HEADS * SB_HEAD_DIM
ML_HEADS = 4
ML_HEAD_DIM = 128
ML_WIDTH = ML_HEADS * ML_HEAD_DIM
ML_CONV = 4
ML_CHUNK = 128
XA_HEADS = 4
XA_HEAD_DIM = 128
XA_WIDTH = XA_HEADS * XA_HEAD_DIM
N_BRANCH = 3
D_FF = 2816
FF_CONV = 3
Q_BLOCK = 128
EPS = 1e-6

IN_SIZES = (SB_WIDTH, SB_WIDTH, SB_WIDTH, ML_WIDTH, ML_WIDTH, ML_WIDTH, ML_WIDTH,
            ML_HEADS, ML_HEADS, XA_WIDTH, N_BRANCH * D_MODEL)
IN_TOTAL = sum(IN_SIZES)

kernel_name = 'stickbreak_mlstm_memxattn_convffn_hybrid'


def _rmsnorm(x, g):
    xf = x.astype(jnp.float32)
    y = xf * lax.rsqrt(jnp.mean(xf * xf, axis=-1, keepdims=True) + EPS)
    return (y * g.astype(jnp.float32)).astype(x.dtype)


def _causal_dwconv(x, w, b):
    k_width, chans = w.shape
    y = lax.conv_general_dilated(
        x, w[:, None, :].astype(x.dtype), window_strides=(1,),
        padding=[(k_width - 1, 0)], dimension_numbers=('NWC', 'WIO', 'NWC'),
        feature_group_count=chans)
    return y + b.astype(x.dtype)


def _split_heads(t, n_heads):
    b, s, _ = t.shape
    return t.reshape(b, s, n_heads, -1).transpose(0, 2, 1, 3)


def _merge_heads(t):
    b, h, s, d = t.shape
    return t.transpose(0, 2, 1, 3).reshape(b, s, h * d)


def _stick_breaking_attention(q, k, v):
    seq, dh = q.shape[2], q.shape[3]
    scale = 1.0 / math.sqrt(dh)
    q_pos = jnp.arange(Q_BLOCK)
    outs = []
    for blk in range(seq // Q_BLOCK):
        t0 = blk * Q_BLOCK
        t1 = t0 + Q_BLOCK
        z = jnp.einsum('bhtd,bhsd->bhts', q[:, :, t0:t1], k[:, :, :t1]).astype(jnp.float32) * scale
        strict = jnp.arange(t1)[None, :] < (t0 + q_pos)[:, None]
        log_fail = jnp.where(strict, jax.nn.log_sigmoid(-z), 0.0)
        log_after = lax.cumsum(log_fail, axis=3, reverse=True) - log_fail
        w = jnp.where(strict, jnp.exp(jax.nn.log_sigmoid(z) + log_after), 0.0)
        outs.append(jnp.einsum('bhts,bhsd->bhtd', w.astype(v.dtype), v[:, :, :t1]))
    return jnp.concatenate(outs, axis=2)


def _mlstm_chunkwise(q, k, v, i_pre, f_pre):
    bsz, nh, seq, dh = q.shape
    n_chunks = seq // ML_CHUNK
    k = k * (1.0 / math.sqrt(dh))

    def chunked(t):
        t = t.reshape(bsz, nh, n_chunks, ML_CHUNK, *t.shape[3:])
        return jnp.moveaxis(t, 2, 0)

    log_f = jax.nn.log_sigmoid(f_pre)
    causal = jnp.tril(jnp.ones((ML_CHUNK, ML_CHUNK), dtype=bool))

    def step(carry, inp):
        c_st, n_st, m_st = carry
        qc, kc, vc, ic, lfc = inp
        b = jnp.cumsum(lfc, axis=-1)
        log_d = jnp.where(causal, b[..., :, None] - b[..., None, :] + ic[..., None, :], -jnp.inf)
        m_inter = b + m_st[..., None]
        m_t = jnp.maximum(m_inter, jnp.max(log_d, axis=-1))
        s = jnp.einsum('bhtd,bhsd->bhts', qc, kc) * jnp.exp(log_d - m_t[..., None])
        w_inter = jnp.exp(m_inter - m_t)
        num = (jnp.einsum('bhts,bhsd->bhtd', s, vc)
               + w_inter[..., None] * jnp.einsum('bhvk,bhtk->bhtv', c_st, qc))
        den = jnp.sum(s, axis=-1) + w_inter * jnp.einsum('bhk,bhtk->bht', n_st, qc)
        h = num / jnp.maximum(jnp.abs(den), jnp.exp(-m_t))[..., None]
        b_last = b[..., -1]
        log_g = b_last[..., None] - b + ic
        m_new = jnp.maximum(b_last + m_st, jnp.max(log_g, axis=-1))
        decay = jnp.exp(b_last + m_st - m_new)
        wk = jnp.exp(log_g - m_new[..., None])
        c_new = decay[..., None, None] * c_st + jnp.einsum('bhs,bhsv,bhsk->bhvk', wk, vc, kc)
        n_new = decay[..., None] * n_st + jnp.einsum('bhs,bhsk->bhk', wk, kc)
        return (c_new, n_new, m_new), h

    init = (jnp.zeros((bsz, nh, dh, dh), jnp.float32),
            jnp.zeros((bsz, nh, dh), jnp.float32),
            jnp.zeros((bsz, nh), jnp.float32))
    _, h = lax.scan(step, init, (chunked(q), chunked(k), chunked(v), chunked(i_pre), chunked(log_f)))
    return jnp.moveaxis(h, 0, 2).reshape(bsz, nh, seq, dh)


def _memory_cross_attention(q, mem_n, w_kv, g_q, g_k):
    k, v = jnp.split(mem_n @ w_kv, 2, axis=-1)
    qh = _rmsnorm(_split_heads(q, XA_HEADS), g_q)
    kh = _rmsnorm(_split_heads(k, XA_HEADS), g_k)
    vh = _split_heads(v, XA_HEADS)
    scores = jnp.einsum('bhsd,bhmd->bhsm', qh, kh).astype(jnp.float32) * (1.0 / math.sqrt(XA_HEAD_DIM))
    p = jax.nn.softmax(scores, axis=-1)
    return _merge_heads(jnp.einsum('bhsm,bhmd->bhsd', p.astype(vh.dtype), vh))


def setup_inputs(seed: int = 0) -> dict:
    key = jax.random.key(seed)
    ks = jax.random.split(key, 24)

    def nrm(k, shape, scale):
        return jax.random.normal(k, shape, jnp.float32) * scale

    def gain(k, shape):
        return 1.0 + 0.02 * jax.random.normal(k, shape, jnp.float32)

    return {
        'x': nrm(ks[0], (BATCH, SEQ, D_MODEL), 1.0),
        'mem': nrm(ks[1], (BATCH, N_MEM, D_MODEL), 1.0),
        'g_mix': gain(ks[2], (DEPTH, D_MODEL)),
        'w_in': nrm(ks[3], (DEPTH, D_MODEL, IN_TOTAL), D_MODEL ** -0.5),
        'ml_conv_w': nrm(ks[4], (DEPTH, ML_CONV, 2 * ML_WIDTH), ML_CONV ** -0.5),
        'ml_conv_b': nrm(ks[5], (DEPTH, 2 * ML_WIDTH), 0.01),
        'ml_b_i': nrm(ks[6], (DEPTH, ML_HEADS), 0.1),
        'ml_b_f': jnp.linspace(3.0, 6.0, ML_HEADS, dtype=jnp.float32)[None, :] + nrm(ks[7], (DEPTH, ML_HEADS), 0.1),
        'ml_g_out': gain(ks[8], (DEPTH, ML_HEADS, ML_HEAD_DIM)),
        'g_mem': gain(ks[9], (DEPTH, D_MODEL)),
        'w_mem_kv': nrm(ks[10], (DEPTH, D_MODEL, 2 * XA_WIDTH), D_MODEL ** -0.5),
        'xa_g_q': gain(ks[11], (DEPTH, XA_HEAD_DIM)),
        'xa_g_k': gain(ks[12], (DEPTH, XA_HEAD_DIM)),
        'w_sb_out': nrm(ks[13], (DEPTH, SB_WIDTH, D_MODEL), SB_WIDTH ** -0.5),
        'w_ml_out': nrm(ks[14], (DEPTH, ML_WIDTH, D_MODEL), ML_WIDTH ** -0.5),
        'w_xa_out': nrm(ks[15], (DEPTH, XA_WIDTH, D_MODEL), XA_WIDTH ** -0.5),
        'w_o': nrm(ks[16], (DEPTH, D_MODEL, D_MODEL), D_MODEL ** -0.5),
        'g_ffn': gain(ks[17], (DEPTH, D_MODEL)),
        'w_up': nrm(ks[18], (DEPTH, D_MODEL, 2 * D_FF), D_MODEL ** -0.5),
        'ff_conv_w': nrm(ks[19], (DEPTH, FF_CONV, 2 * D_FF), FF_CONV ** -0.5),
        'ff_conv_b': nrm(ks[20], (DEPTH, 2 * D_FF), 0.01),
        'w_down': nrm(ks[21], (DEPTH, D_FF, D_MODEL), D_FF ** -0.5),
    }


def reference(x, mem, g_mix, w_in, ml_conv_w, ml_conv_b, ml_b_i, ml_b_f, ml_g_out, g_mem,
              w_mem_kv, xa_g_q, xa_g_k, w_sb_out, w_ml_out, w_xa_out, w_o, g_ffn, w_up,
              ff_conv_w, ff_conv_b, w_down):
    bsz, seq, _ = x.shape
    offsets = np.cumsum(IN_SIZES)[:-1].tolist()
    for l in range(DEPTH):
        h = _rmsnorm(x, g_mix[l])
        (sb_q, sb_k, sb_v, ml_q, ml_k, ml_v, ml_o, ml_i, ml_f, xa_q, gate_pre) = jnp.split(
            h @ w_in[l], offsets, axis=-1)

        y_sb = _merge_heads(_stick_breaking_attention(
            _split_heads(sb_q, SB_HEADS), _split_heads(sb_k, SB_HEADS), _split_heads(sb_v, SB_HEADS)))

        ml_qk = jax.nn.silu(_causal_dwconv(jnp.concatenate([ml_q, ml_k], axis=-1), ml_conv_w[l], ml_conv_b[l]))
        ml_q, ml_k = jnp.split(ml_qk, 2, axis=-1)
        i_pre = (ml_i + ml_b_i[l]).astype(jnp.float32).transpose(0, 2, 1)
        f_pre = (ml_f + ml_b_f[l]).astype(jnp.float32).transpose(0, 2, 1)
        h_ml = _mlstm_chunkwise(
            _split_heads(ml_q, ML_HEADS).astype(jnp.float32),
            _split_heads(ml_k, ML_HEADS).astype(jnp.float32),
            _split_heads(ml_v, ML_HEADS).astype(jnp.float32), i_pre, f_pre)
        h_ml = _merge_heads(h_ml).astype(x.dtype).reshape(bsz, seq, ML_HEADS, ML_HEAD_DIM)
        h_ml = _rmsnorm(h_ml, ml_g_out[l]).reshape(bsz, seq, ML_WIDTH)
        y_ml = jax.nn.sigmoid(ml_o) * h_ml

        y_xa = _memory_cross_attention(xa_q, _rmsnorm(mem, g_mem[l]), w_mem_kv[l], xa_g_q[l], xa_g_k[l])

        gates = jax.nn.sigmoid(gate_pre).reshape(bsz, seq, N_BRANCH, D_MODEL)
        merged = (gates[:, :, 0] * (y_sb @ w_sb_out[l])
                  + gates[:, :, 1] * (y_ml @ w_ml_out[l])
                  + gates[:, :, 2] * (y_xa @ w_xa_out[l]))
        x = x + merged @ w_o[l]

        h = _rmsnorm(x, g_ffn[l])
        u = _causal_dwconv(h @ w_up[l], ff_conv_w[l], ff_conv_b[l])
        u_val, u_gate = jnp.split(u, 2, axis=-1)
        x = x + (jax.nn.silu(u_gate) * u_val) @ w_down[l]
    return x
```

```python
import functools
import math

import jax
import jax.numpy as jnp
from jax import lax
from jax.experimental import pallas as pl
from jax.experimental.pallas import tpu as pltpu

F32 = jnp.float32
BF16 = jnp.bfloat16

SB_HEADS = 8
SB_HEAD_DIM = 64
ML_HEADS = 4
ML_HEAD_DIM = 128
ML_CONV = 4
ML_CHUNK = 128
XA_HEADS = 4
XA_HEAD_DIM = 128
FF_CONV = 3
EPS = 1e-6

LANES = 128
SUBLANES = 8
VMEM_LIMIT = 56 * 1024 * 1024


def _cparams(sem):
    return pltpu.CompilerParams(dimension_semantics=sem, vmem_limit_bytes=VMEM_LIMIT)


def _dot(a, b):
    return jnp.dot(a, b, preferred_element_type=F32)


def _dot_nt(a, b):
    return lax.dot_general(a, b, (((1,), (1,)), ((), ())), preferred_element_type=F32)


def _dot_tn(a, b):
    return lax.dot_general(a, b, (((0,), (0,)), ((), ())), preferred_element_type=F32)


def _sigmoid(x):
    return 1.0 / (1.0 + jnp.exp(-x))


def _inproj_kernel(x_ref, g_ref, w_ref, wif_ref, wift_ref, o_ref, ifc_ref, ifr_ref, h_scr):
    @pl.when(pl.program_id(1) == 0)
    def _():
        x = x_ref[...]
        ms = jnp.mean(x * x, axis=-1, keepdims=True)
        h = (x * lax.rsqrt(ms + EPS) * g_ref[...]).astype(BF16)
        h_scr[...] = h
        ifc_ref[...] = _dot(h, wif_ref[...])
        ifr_ref[...] = _dot_nt(wift_ref[...], h)

    o_ref[...] = _dot(h_scr[...], w_ref[...]).astype(BF16)


def _inproj(x2, g, w_main, w_if, w_ift, tm, tn):
    t, d = x2.shape
    n = w_main.shape[1]
    return pl.pallas_call(
        _inproj_kernel,
        grid=(t // tm, n // tn),
        in_specs=[
            pl.BlockSpec((tm, d), lambda i, j: (i, 0)),
            pl.BlockSpec((1, d), lambda i, j: (0, 0)),
            pl.BlockSpec((d, tn), lambda i, j: (0, j)),
            pl.BlockSpec((d, LANES), lambda i, j: (0, 0)),
            pl.BlockSpec((SUBLANES, d), lambda i, j: (0, 0)),
        ],
        out_specs=[
            pl.BlockSpec((tm, tn), lambda i, j: (i, j)),
            pl.BlockSpec((tm, LANES), lambda i, j: (i, 0)),
            pl.BlockSpec((SUBLANES, tm), lambda i, j: (0, i)),
        ],
        out_shape=[
            jax.ShapeDtypeStruct((t, n), BF16),
            jax.ShapeDtypeStruct((t, LANES), F32),
            jax.ShapeDtypeStruct((SUBLANES, t), F32),
        ],
        scratch_shapes=[pltpu.VMEM((tm, d), BF16)],
        compiler_params=_cparams(("arbitrary", "arbitrary")),
        name="inproj",
    )(x2, g, w_main, w_if, w_ift)


def _sb_kernel(q_ref, k_ref, v_ref, o_ref, *, tq):
    i = pl.program_id(2)
    q = q_ref[...]
    lane = lax.broadcasted_iota(jnp.int32, (tq, LANES), 1)
    row = lax.broadcasted_iota(jnp.int32, (tq, tq), 0)
    col = lax.broadcasted_iota(jnp.int32, (tq, tq), 1)
    strict = col < row
    r2 = lax.broadcasted_iota(jnp.int32, (tq, 2 * tq), 0)
    c2 = lax.broadcasted_iota(jnp.int32, (tq, 2 * tq), 1)
    suffix_ones = jnp.where((r2 > c2) | (c2 >= tq), 1.0, 0.0).astype(BF16)

    def block(kb, q_h, acc, carry, masked):
        start = pl.multiple_of(kb * tq, tq)
        k = k_ref[pl.ds(start, tq), :]
        v = v_ref[pl.ds(start, tq), :]
        nz = _dot_nt(q_h, k)
        log_fail = jnp.minimum(nz, 0.0) - jnp.log(1.0 + jnp.exp(-jnp.abs(nz)))
        if masked:
            log_fail = jnp.where(strict, log_fail, 0.0)
        hi = log_fail.astype(BF16)
        lo = (log_fail - hi.astype(F32)).astype(BF16)
        sums = _dot(hi, suffix_ones) + _dot(lo, suffix_ones)
        log_after = sums[:, :tq] + carry
        w = jnp.exp(log_fail - nz + log_after)
        if masked:
            w = jnp.where(strict, w, 0.0)
        acc = acc + _dot(w.astype(BF16), v)
        carry = carry + sums[:, tq:]
        return acc, carry

    def one_head(q_h):
        zero = jnp.zeros((tq, LANES), F32)
        acc, carry = block(i, q_h, zero, zero, True)

        def body(it, c):
            return block(i - 1 - it, q_h, c[0], c[1], False)

        acc, carry = lax.fori_loop(0, i, body, (acc, carry))
        return acc

    acc0 = one_head(jnp.where(lane < SB_HEAD_DIM, q, jnp.zeros_like(q)))
    acc1 = one_head(jnp.where(lane >= SB_HEAD_DIM, q, jnp.zeros_like(q)))
    o_ref[...] = jnp.where(lane < SB_HEAD_DIM, acc0, acc1).astype(BF16)


def _sb_attention(proj, bsz, seq, qcol, kcol, vcol, tq):
    npair = SB_HEADS * SB_HEAD_DIM // LANES
    nq = seq // tq
    return pl.pallas_call(
        functools.partial(_sb_kernel, tq=tq),
        grid=(bsz, npair, nq),
        in_specs=[
            pl.BlockSpec((tq, LANES), lambda b, p, i: (b * nq + i, qcol + p)),
            pl.BlockSpec((seq, LANES), lambda b, p, i: (b, kcol + p)),
            pl.BlockSpec((seq, LANES), lambda b, p, i: (b, vcol + p)),
        ],
        out_specs=pl.BlockSpec((tq, LANES), lambda b, p, i: (b * nq + i, p)),
        out_shape=jax.ShapeDtypeStruct((bsz * seq, npair * LANES), BF16),
        compiler_params=_cparams(("arbitrary", "arbitrary", "arbitrary")),
        name="sb_attn",
    )(proj, proj, proj)


def _mlstm_kernel(q_ref, k_ref, v_ref, o_ref, ifc_ref, ifr_ref, cw_ref, cb_ref,
                  bic_ref, bir_ref, gout_ref, y_ref,
                  qbuf, kbuf, ct_scr, m_scr):
    c = pl.program_id(1)
    L = ML_CHUNK
    W = ML_HEADS * ML_HEAD_DIM

    @pl.when(c == 0)
    def _():
        qbuf[0:SUBLANES, :] = jnp.zeros((SUBLANES, W), F32)
        kbuf[0:SUBLANES, :] = jnp.zeros((SUBLANES, W), F32)
        ct_scr[...] = jnp.zeros_like(ct_scr)
        m_scr[...] = jnp.zeros_like(m_scr)

    qbuf[SUBLANES:SUBLANES + L, :] = q_ref[...].astype(F32)
    kbuf[SUBLANES:SUBLANES + L, :] = k_ref[...].astype(F32)

    def conv_silu(buf, w, b):
        y = b
        for j in range(ML_CONV):
            off = SUBLANES - (ML_CONV - 1) + j
            y = y + w[j:j + 1, :] * buf[off:off + L, :]
        return y * _sigmoid(y)

    cw = cw_ref[...]
    cb = cb_ref[...]
    qc = conv_silu(qbuf, cw[:, :W], cb[:, :W])
    kc = conv_silu(kbuf, cw[:, W:], cb[:, W:]) * (1.0 / math.sqrt(ML_HEAD_DIM))
    qbuf[0:SUBLANES, :] = qbuf[L:L + SUBLANES, :]
    kbuf[0:SUBLANES, :] = kbuf[L:L + SUBLANES, :]

    row = lax.broadcasted_iota(jnp.int32, (L, L), 0)
    col = lax.broadcasted_iota(jnp.int32, (L, L), 1)
    causal = col <= row
    ifc = ifc_ref[...] + bic_ref[...]
    ifr = ifr_ref[...] + bir_ref[...]
    ones = jnp.ones((L, LANES), BF16)

    for h in range(ML_HEADS):
        sl = slice(h * ML_HEAD_DIM, (h + 1) * ML_HEAD_DIM)
        i_col = ifc[:, h:h + 1]
        f_col = ifc[:, ML_HEADS + h:ML_HEADS + h + 1]
        i_row = ifr[h:h + 1, :]
        f_row = ifr[ML_HEADS + h:ML_HEADS + h + 1, :]
        lf_col = jnp.minimum(f_col, 0.0) - jnp.log(1.0 + jnp.exp(-jnp.abs(f_col)))
        lf_row = jnp.minimum(f_row, 0.0) - jnp.log(1.0 + jnp.exp(-jnp.abs(f_row)))
        b_col = jnp.sum(jnp.where(causal, lf_row, 0.0), axis=1, keepdims=True)
        b_row = jnp.sum(jnp.where(row <= col, lf_col, 0.0), axis=0, keepdims=True)
        b_last = jnp.sum(lf_row, axis=1, keepdims=True)
        m_st = m_scr[h, 0:1, 0:1]

        log_d = jnp.where(causal, b_col - b_row + i_row, -jnp.inf)
        m_inter = b_col + m_st
        m_t = jnp.maximum(m_inter, jnp.max(log_d, axis=1, keepdims=True))
        q_h = qc[:, sl].astype(BF16)
        k_h = kc[:, sl]
        v_ext = jnp.concatenate([v_ref[:, sl], ones], axis=1)
        s = _dot_nt(q_h, k_h.astype(BF16)) * jnp.exp(log_d - m_t)
        w_inter = jnp.exp(m_inter - m_t)
        ct = ct_scr[h]
        num = _dot(s.astype(BF16), v_ext) + w_inter * _dot(q_h, ct.astype(BF16))
        den = num[:, ML_HEAD_DIM:]
        hh = num[:, :ML_HEAD_DIM] / jnp.maximum(jnp.abs(den), jnp.exp(-m_t))

        log_g = b_last - b_col + i_col
        m_new = jnp.maximum(b_last + m_st, jnp.max(log_g, axis=0, keepdims=True))
        decay = jnp.exp(b_last + m_st - m_new)
        wk = jnp.exp(log_g - m_new)
        ct_scr[h] = decay * ct + _dot_tn((wk * k_h).astype(BF16), v_ext)
        m_scr[h] = jnp.broadcast_to(m_new, (SUBLANES, LANES))

        hn = hh * lax.rsqrt(jnp.mean(hh * hh, axis=-1, keepdims=True) + EPS) * gout_ref[:, sl]
        y_ref[:, sl] = (_sigmoid(o_ref[:, sl].astype(F32)) * hn).astype(BF16)


def _mlstm(proj, ifc, ifr, conv_w, conv_b, bias_col, bias_row, g_out, bsz, seq, cols):
    qcol, kcol, vcol, ocol = cols
    L = ML_CHUNK
    W = ML_HEADS * ML_HEAD_DIM
    nc = seq // L

    def colspec(cb):
        return pl.BlockSpec((L, W), lambda b, c: (b * nc + c, cb))

    def const(shape):
        return pl.BlockSpec(shape, lambda b, c: (0,) * len(shape))

    return pl.pallas_call(
        _mlstm_kernel,
        grid=(bsz, nc),
        in_specs=[
            colspec(qcol), colspec(kcol), colspec(vcol), colspec(ocol),
            pl.BlockSpec((L, LANES), lambda b, c: (b * nc + c, 0)),
            pl.BlockSpec((SUBLANES, L), lambda b, c: (0, b * nc + c)),
            const((ML_CONV, 2 * W)), const((1, 2 * W)),
            const((1, LANES)), const((SUBLANES, 1)), const((1, W)),
        ],
        out_specs=pl.BlockSpec((L, W), lambda b, c: (b * nc + c, 0)),
        out_shape=jax.ShapeDtypeStruct((bsz * seq, W), BF16),
        scratch_shapes=[
            pltpu.VMEM((L + 2 * SUBLANES, W), F32),
            pltpu.VMEM((L + 2 * SUBLANES, W), F32),
            pltpu.VMEM((ML_HEADS, ML_HEAD_DIM, ML_HEAD_DIM + LANES), F32),
            pltpu.VMEM((ML_HEADS, SUBLANES, LANES), F32),
        ],
        compiler_params=_cparams(("arbitrary", "arbitrary")),
        name="mlstm",
    )(proj, proj, proj, proj, ifc, ifr, conv_w, conv_b, bias_col, bias_row, g_out)


def _memkv_kernel(mem_ref, g_ref, w_ref, gk_ref, k_ref, v_ref):
    x = mem_ref[...]
    ms = jnp.mean(x * x, axis=-1, keepdims=True)
    mn = (x * lax.rsqrt(ms + EPS) * g_ref[...]).astype(BF16)
    kv = _dot(mn, w_ref[...])
    W = XA_HEADS * XA_HEAD_DIM
    for h in range(XA_HEADS):
        sl = slice(h * XA_HEAD_DIM, (h + 1) * XA_HEAD_DIM)
        kh = kv[:, sl]
        kn = kh * lax.rsqrt(jnp.mean(kh * kh, axis=-1, keepdims=True) + EPS) * gk_ref[...]
        k_ref[:, sl] = kn.astype(BF16)
    v_ref[...] = kv[:, W:].astype(BF16)


def _memkv(mem2, g_mem, w_kv, g_k, bsz, n_mem):
    d = mem2.shape[1]
    W = XA_HEADS * XA_HEAD_DIM
    return pl.pallas_call(
        _memkv_kernel,
        grid=(bsz,),
        in_specs=[
            pl.BlockSpec((n_mem, d), lambda b: (b, 0)),
            pl.BlockSpec((1, d), lambda b: (0, 0)),
            pl.BlockSpec((d, 2 * W), lambda b: (0, 0)),
            pl.BlockSpec((1, XA_HEAD_DIM), lambda b: (0, 0)),
        ],
        out_specs=[pl.BlockSpec((n_mem, W), lambda b: (b, 0)),
                   pl.BlockSpec((n_mem, W), lambda b: (b, 0))],
        out_shape=[jax.ShapeDtypeStruct((bsz * n_mem, W), BF16)] * 2,
        compiler_params=_cparams(("arbitrary",)),
        name="memkv",
    )(mem2, g_mem, w_kv, g_k)


def _xattn_kernel(q_ref, k_ref, v_ref, gq_ref, y_ref):
    scale = 1.0 / math.sqrt(XA_HEAD_DIM)
    for h in range(XA_HEADS):
        sl = slice(h * XA_HEAD_DIM, (h + 1) * XA_HEAD_DIM)
        qh = q_ref[:, sl].astype(F32)
        qn = qh * lax.rsqrt(jnp.mean(qh * qh, axis=-1, keepdims=True) + EPS) * gq_ref[...]
        s = _dot_nt(qn.astype(BF16), k_ref[:, sl]) * scale
        p = jnp.exp(s - jnp.max(s, axis=-1, keepdims=True))
        p = p / jnp.sum(p, axis=-1, keepdims=True)
        y_ref[:, sl] = _dot(p.astype(BF16), v_ref[:, sl]).astype(BF16)


def _xattn(proj, kn, vv, g_q, bsz, seq, n_mem, qcol, tm):
    W = XA_HEADS * XA_HEAD_DIM
    nt = seq // tm
    return pl.pallas_call(
        _xattn_kernel,
        grid=(bsz, nt),
        in_specs=[
            pl.BlockSpec((tm, W), lambda b, i: (b * nt + i, qcol)),
            pl.BlockSpec((n_mem, W), lambda b, i: (b, 0)),
            pl.BlockSpec((n_mem, W), lambda b, i: (b, 0)),
            pl.BlockSpec((1, XA_HEAD_DIM), lambda b, i: (0, 0)),
        ],
        out_specs=pl.BlockSpec((tm, W), lambda b, i: (b * nt + i, 0)),
        out_shape=jax.ShapeDtypeStruct((bsz * seq, W), BF16),
        compiler_params=_cparams(("arbitrary", "arbitrary")),
        name="xattn",
    )(proj, kn, vv, g_q)


def _merge_kernel(x_ref, ysb_ref, yml_ref, yxa_ref, g0_ref, g1_ref, g2_ref,
                  wsb_ref, wml_ref, wxa_ref, wo_ref, gf_ref, x1_ref, h2_ref):
    merged = _sigmoid(g0_ref[...].astype(F32)) * _dot(ysb_ref[...], wsb_ref[...])
    merged = merged + _sigmoid(g1_ref[...].astype(F32)) * _dot(yml_ref[...], wml_ref[...])
    merged = merged + _sigmoid(g2_ref[...].astype(F32)) * _dot(yxa_ref[...], wxa_ref[...])
    x1 = x_ref[...] + _dot(merged.astype(BF16), wo_ref[...])
    x1_ref[...] = x1
    ms = jnp.mean(x1 * x1, axis=-1, keepdims=True)
    h2_ref[...] = (x1 * lax.rsqrt(ms + EPS) * gf_ref[...]).astype(BF16)


def _merge(x2, ysb, yml, yxa, proj, gcol, w_sb, w_ml, w_xa, w_o, g_ffn, tm):
    t, d = x2.shape
    wb = ysb.shape[1]

    def rows(width, cb=0):
        return pl.BlockSpec((tm, width), lambda i: (i, cb))

    def const(shape):
        return pl.BlockSpec(shape, lambda i: (0, 0))

    return pl.pallas_call(
        _merge_kernel,
        grid=(t // tm,),
        in_specs=[
            rows(d), rows(wb), rows(wb), rows(wb),
            rows(d, gcol), rows(d, gcol + 1), rows(d, gcol + 2),
            const((wb, d)), const((wb, d)), const((wb, d)), const((d, d)), const((1, d)),
        ],
        out_specs=[rows(d), rows(d)],
        out_shape=[jax.ShapeDtypeStruct((t, d), F32), jax.ShapeDtypeStruct((t, d), BF16)],
        compiler_params=_cparams(("arbitrary",)),
        name="merge",
    )(x2, ysb, yml, yxa, proj, proj, proj, w_sb, w_ml, w_xa, w_o, g_ffn)


def _ffn_kernel(x1_ref, h2_ref, wv_ref, wg_ref, cwv_ref, cwg_ref, cbv_ref, cbg_ref, wd_ref,
                o_ref, vbuf, gbuf, vtail, gtail, acc, *, tm, tiles_per_seq):
    i = pl.program_id(0)
    j = pl.program_id(1)
    nj = pl.num_programs(1)

    @pl.when(i % tiles_per_seq == 0)
    def _():
        vtail[j] = jnp.zeros(vtail.shape[1:], F32)
        gtail[j] = jnp.zeros(gtail.shape[1:], F32)

    h2 = h2_ref[...]

    def conv(buf, tail, w_up_ref, cw_ref, cb_ref):
        buf[0:SUBLANES, :] = tail[j]
        buf[SUBLANES:SUBLANES + tm, :] = _dot(h2, w_up_ref[...])
        tail[j] = buf[tm:tm + SUBLANES, :]
        cw = cw_ref[...]
        y = cb_ref[...]
        for t in range(FF_CONV):
            off = SUBLANES - (FF_CONV - 1) + t
            y = y + cw[t:t + 1, :] * buf[off:off + tm, :]
        return y

    uv = conv(vbuf, vtail, wv_ref, cwv_ref, cbv_ref)
    ug = conv(gbuf, gtail, wg_ref, cwg_ref, cbg_ref)
    act = (ug * _sigmoid(ug) * uv).astype(BF16)
    part = _dot(act, wd_ref[...])

    @pl.when(j == 0)
    def _():
        acc[...] = x1_ref[...] + part

    @pl.when(j > 0)
    def _():
        acc[...] += part

    @pl.when(j == nj - 1)
    def _():
        o_ref[...] = acc[...]


def _ffn(x1, h2, w_up, conv_w, conv_b, w_down, seq, tm, tf):
    t, d = x1.shape
    dff = w_down.shape[0]
    nj = dff // tf
    return pl.pallas_call(
        functools.partial(_ffn_kernel, tm=tm, tiles_per_seq=seq // tm),
        grid=(t // tm, nj),
        in_specs=[
            pl.BlockSpec((tm, d), lambda i, j: (i, 0)),
            pl.BlockSpec((tm, d), lambda i, j: (i, 0)),
            pl.BlockSpec((d, tf), lambda i, j: (0, j)),
            pl.BlockSpec((d, tf), lambda i, j: (0, nj + j)),
            pl.BlockSpec((FF_CONV, tf), lambda i, j: (0, j)),
            pl.BlockSpec((FF_CONV, tf), lambda i, j: (0, nj + j)),
            pl.BlockSpec((1, tf), lambda i, j: (0, j)),
            pl.BlockSpec((1, tf), lambda i, j: (0, nj + j)),
            pl.BlockSpec((tf, d), lambda i, j: (j, 0)),
        ],
        out_specs=pl.BlockSpec((tm, d), lambda i, j: (i, 0)),
        out_shape=jax.ShapeDtypeStruct((t, d), F32),
        scratch_shapes=[
            pltpu.VMEM((tm + SUBLANES, tf), F32),
            pltpu.VMEM((tm + SUBLANES, tf), F32),
            pltpu.VMEM((nj, SUBLANES, tf), F32),
            pltpu.VMEM((nj, SUBLANES, tf), F32),
            pltpu.VMEM((tm, d), F32),
        ],
        compiler_params=_cparams(("arbitrary", "arbitrary")),
        name="ffn",
    )(x1, h2, w_up, w_up, conv_w, conv_w, conv_b, conv_b, w_down)


def kernel(x, mem, g_mix, w_in, ml_conv_w, ml_conv_b, ml_b_i, ml_b_f, ml_g_out, g_mem,
           w_mem_kv, xa_g_q, xa_g_k, w_sb_out, w_ml_out, w_xa_out, w_o, g_ffn, w_up,
           ff_conv_w, ff_conv_b, w_down):
    bsz, seq, d = x.shape
    n_mem = mem.shape[1]
    depth = w_in.shape[0]
    sbw = SB_HEADS * SB_HEAD_DIM
    mlw = ML_HEADS * ML_HEAD_DIM
    xaw = XA_HEADS * XA_HEAD_DIM
    n_wide = 3 * sbw + 4 * mlw
    if_lo, if_hi = n_wide, n_wide + 2 * ML_HEADS
    sb_scale = 1.0 / math.sqrt(SB_HEAD_DIM)

    x2 = x.reshape(bsz * seq, d)
    mem2 = mem.reshape(bsz * n_mem, d)
    for l in range(depth):
        wl = w_in[l]
        w_main = jnp.concatenate(
            [wl[:, :sbw] * (-sb_scale), wl[:, sbw:n_wide], wl[:, if_hi:]], axis=1).astype(BF16)
        w_if = wl[:, if_lo:if_hi]
        w_ifc = jnp.pad(w_if, ((0, 0), (0, LANES - 2 * ML_HEADS))).astype(BF16)
        w_ifr = w_if.T.astype(BF16)
        proj, ifc, ifr = _inproj(x2, g_mix[l][None, :], w_main, w_ifc, w_ifr, tm=512, tn=1024)

        ysb = _sb_attention(proj, bsz, seq, qcol=0, kcol=sbw // LANES, vcol=2 * sbw // LANES, tq=128)

        bias = jnp.concatenate([ml_b_i[l], ml_b_f[l]])
        bias_col = jnp.pad(bias, (0, LANES - 2 * ML_HEADS))[None, :]
        bias_row = bias[:, None]
        ml0 = 3 * sbw // mlw
        yml = _mlstm(proj, ifc, ifr, ml_conv_w[l], ml_conv_b[l][None, :], bias_col, bias_row,
                     ml_g_out[l].reshape(1, mlw), bsz, seq, cols=(ml0, ml0 + 1, ml0 + 2, ml0 + 3))

        kn, vv = _memkv(mem2, g_mem[l][None, :], w_mem_kv[l].astype(BF16), xa_g_k[l][None, :], bsz, n_mem)
        yxa = _xattn(proj, kn, vv, xa_g_q[l][None, :], bsz, seq, n_mem, qcol=n_wide // xaw, tm=512)

        x1, h2 = _merge(x2, ysb, yml, yxa, proj, (n_wide + xaw) // d,
                        w_sb_out[l].astype(BF16), w_ml_out[l].astype(BF16), w_xa_out[l].astype(BF16),
                        w_o[l].astype(BF16), g_ffn[l][None, :], tm=512)

        x2 = _ffn(x1, h2, w_up[l].astype(BF16), ff_conv_w[l], ff_conv_b[l][None, :],
                  w_down[l].astype(BF16), seq, tm=512, tf=256)
    return x2.reshape(bsz, seq, d)
```

```python
import functools
import math

import jax
import jax.numpy as jnp
from jax import lax
from jax.experimental import pallas as pl
from jax.experimental.pallas import tpu as pltpu

F32 = jnp.float32
BF16 = jnp.bfloat16

SB_HEADS = 8
SB_HEAD_DIM = 64
ML_HEADS = 4
ML_HEAD_DIM = 128
ML_CONV = 4
ML_CHUNK = 128
XA_HEADS = 4
XA_HEAD_DIM = 128
FF_CONV = 3
EPS = 1e-6
SB_UNROLL = 2

LANES = 128
SUBLANES = 8
VMEM_LIMIT = 56 * 1024 * 1024


def _cparams(sem):
    return pltpu.CompilerParams(dimension_semantics=sem, vmem_limit_bytes=VMEM_LIMIT)


def _dot(a, b):
    return jnp.dot(a, b, preferred_element_type=F32)


def _dot_nt(a, b):
    return lax.dot_general(a, b, (((1,), (1,)), ((), ())), preferred_element_type=F32)


def _dot_tn(a, b):
    return lax.dot_general(a, b, (((0,), (0,)), ((), ())), preferred_element_type=F32)


def _sigmoid(x):
    return 1.0 / (1.0 + jnp.exp(-x))


def _inproj_kernel(x_ref, g_ref, w_ref, wif_ref, wift_ref, o_ref, ifc_ref, ifr_ref, h_scr):
    @pl.when(pl.program_id(1) == 0)
    def _():
        x = x_ref[...]
        ms = jnp.mean(x * x, axis=-1, keepdims=True)
        h = (x * lax.rsqrt(ms + EPS) * g_ref[...]).astype(BF16)
        h_scr[...] = h
        ifc_ref[...] = _dot(h, wif_ref[...])
        ifr_ref[...] = _dot_nt(wift_ref[...], h)

    o_ref[...] = _dot(h_scr[...], w_ref[...]).astype(BF16)


def _inproj(x2, g, w_main, w_if, w_ift, tm, tn):
    t, d = x2.shape
    n = w_main.shape[1]
    return pl.pallas_call(
        _inproj_kernel,
        grid=(t // tm, n // tn),
        in_specs=[
            pl.BlockSpec((tm, d), lambda i, j: (i, 0)),
            pl.BlockSpec((1, d), lambda i, j: (0, 0)),
            pl.BlockSpec((d, tn), lambda i, j: (0, j)),
            pl.BlockSpec((d, LANES), lambda i, j: (0, 0)),
            pl.BlockSpec((SUBLANES, d), lambda i, j: (0, 0)),
        ],
        out_specs=[
            pl.BlockSpec((tm, tn), lambda i, j: (i, j)),
            pl.BlockSpec((tm, LANES), lambda i, j: (i, 0)),
            pl.BlockSpec((SUBLANES, tm), lambda i, j: (0, i)),
        ],
        out_shape=[
            jax.ShapeDtypeStruct((t, n), BF16),
            jax.ShapeDtypeStruct((t, LANES), F32),
            jax.ShapeDtypeStruct((SUBLANES, t), F32),
        ],
        scratch_shapes=[pltpu.VMEM((tm, d), BF16)],
        compiler_params=_cparams(("arbitrary", "arbitrary")),
        name="inproj",
    )(x2, g, w_main, w_if, w_ift)


def _sb_kernel(q_ref, k_ref, v_ref, o_ref, kk_scr, vv_scr, acc_scr, carry_scr, *, tq):
    i = pl.program_id(2)
    tk = LANES
    nsub = tq // tk
    seq = k_ref.shape[0]

    @pl.when(i == 0)
    def _():
        lane = lax.broadcasted_iota(jnp.int32, (seq, LANES), 1)
        for src, dst in ((k_ref, kk_scr), (v_ref, vv_scr)):
            a = src[...]
            zero = jnp.zeros_like(a)
            dst[:, 0:tk, :] = jnp.where(lane < SB_HEAD_DIM, a, zero).reshape(seq // tk, tk, LANES)
            dst[:, tk:2 * tk, :] = jnp.where(lane >= SB_HEAD_DIM, a, zero).reshape(seq // tk, tk, LANES)

    r4 = lax.broadcasted_iota(jnp.int32, (4 * tk, 2 * tk), 0)
    c4 = lax.broadcasted_iota(jnp.int32, (4 * tk, 2 * tk), 1)
    same_head = ((r4 // tk) % 2) == (c4 // tk)
    suffix = jnp.where(same_head & ((r4 % tk) > (c4 % tk)), 1.0, 0.0).astype(BF16)

    acc_scr[...] = jnp.zeros_like(acc_scr)
    carry_scr[...] = jnp.zeros_like(carry_scr)

    def step(kb_lo, nblk, r0, diag_c):
        n = tq - r0
        q = q_ref[r0:tq, :]
        kk = kk_scr[pl.ds(kb_lo, nblk)].reshape(nblk * 2 * tk, LANES)
        y = _dot_nt(q, kk)
        neg_abs = lax.bitcast_convert_type(
            lax.bitcast_convert_type(y, jnp.uint32) | jnp.uint32(0x80000000), F32)
        log_fail = jnp.minimum(y, 0.0) - jnp.log2(1.0 + jnp.exp2(neg_abs))
        if diag_c is not None:
            rr = lax.broadcasted_iota(jnp.int32, (n, 2 * tk), 0) + r0
            cc = lax.broadcasted_iota(jnp.int32, (n, 2 * tk), 1) % tk + diag_c * tk
            strict = cc < rr
            log_fail = jnp.where(strict, log_fail, 0.0)
        log_hit = log_fail - y
        carry = carry_scr[r0:tq, :]
        ws = [None] * nblk
        for b in reversed(range(nblk)):
            sl = slice(b * 2 * tk, (b + 1) * 2 * tk)
            lf = log_fail[:, sl]
            hi = lf.astype(BF16)
            lo = (lf - hi.astype(F32)).astype(BF16)
            suf = _dot(jnp.concatenate([hi, lo], axis=1), suffix)
            w = jnp.exp2(log_hit[:, sl] + (suf + carry))
            if diag_c is not None:
                w = jnp.where(strict, w, 0.0)
            ws[b] = w.astype(BF16)
            tot0 = jnp.sum(lf[:, :tk], axis=1, keepdims=True)
            tot1 = jnp.sum(lf[:, tk:], axis=1, keepdims=True)
            carry = carry + jnp.concatenate(
                [jnp.broadcast_to(tot0, (n, tk)), jnp.broadcast_to(tot1, (n, tk))], axis=1)
        vv = vv_scr[pl.ds(kb_lo, nblk)].reshape(nblk * 2 * tk, LANES)
        acc_scr[r0:tq, :] += _dot(jnp.concatenate(ws, axis=1), vv)
        carry_scr[r0:tq, :] = carry

    for c in reversed(range(nsub)):
        step(i * nsub + c, 1, c * tk, c)

    def body(it, _):
        step(i * nsub - SB_UNROLL * (it + 1), SB_UNROLL, 0, None)
        return 0

    lax.fori_loop(0, i * nsub // SB_UNROLL, body, 0)
    o_ref[...] = acc_scr[...].astype(BF16)


def _sb_attention(proj, bsz, seq, qcol, kcol, vcol, tq):
    npair = SB_HEADS * SB_HEAD_DIM // LANES
    nq = seq // tq
    return pl.pallas_call(
        functools.partial(_sb_kernel, tq=tq),
        grid=(bsz, npair, nq),
        in_specs=[
            pl.BlockSpec((tq, LANES), lambda b, p, i: (b * nq + i, qcol + p)),
            pl.BlockSpec((seq, LANES), lambda b, p, i: (b, kcol + p)),
            pl.BlockSpec((seq, LANES), lambda b, p, i: (b, vcol + p)),
        ],
        out_specs=pl.BlockSpec((tq, LANES), lambda b, p, i: (b * nq + i, p)),
        out_shape=jax.ShapeDtypeStruct((bsz * seq, npair * LANES), BF16),
        scratch_shapes=[
            pltpu.VMEM((seq // LANES, 2 * LANES, LANES), BF16),
            pltpu.VMEM((seq // LANES, 2 * LANES, LANES), BF16),
            pltpu.VMEM((tq, LANES), F32),
            pltpu.VMEM((tq, 2 * LANES), F32),
        ],
        compiler_params=_cparams(("arbitrary", "arbitrary", "arbitrary")),
        name="sb_attn",
    )(proj, proj, proj)


def _mlstm_kernel(q_ref, k_ref, v_ref, o_ref, ifc_ref, ifr_ref, cw_ref, cb_ref,
                  bic_ref, bir_ref, gout_ref, y_ref,
                  qbuf, kbuf, ct_scr, m_scr):
    c = pl.program_id(1)
    L = ML_CHUNK
    W = ML_HEADS * ML_HEAD_DIM

    @pl.when(c == 0)
    def _():
        qbuf[0:SUBLANES, :] = jnp.zeros((SUBLANES, W), F32)
        kbuf[0:SUBLANES, :] = jnp.zeros((SUBLANES, W), F32)
        ct_scr[...] = jnp.zeros_like(ct_scr)
        m_scr[...] = jnp.zeros_like(m_scr)

    qbuf[SUBLANES:SUBLANES + L, :] = q_ref[...].astype(F32)
    kbuf[SUBLANES:SUBLANES + L, :] = k_ref[...].astype(F32)

    def conv_silu(buf, w, b):
        y = b
        for j in range(ML_CONV):
            off = SUBLANES - (ML_CONV - 1) + j
            y = y + w[j:j + 1, :] * buf[off:off + L, :]
        return y * _sigmoid(y)

    cw = cw_ref[...]
    cb = cb_ref[...]
    qc = conv_silu(qbuf, cw[:, :W], cb[:, :W])
    kc = conv_silu(kbuf, cw[:, W:], cb[:, W:]) * (1.0 / math.sqrt(ML_HEAD_DIM))
    qbuf[0:SUBLANES, :] = qbuf[L:L + SUBLANES, :]
    kbuf[0:SUBLANES, :] = kbuf[L:L + SUBLANES, :]

    row = lax.broadcasted_iota(jnp.int32, (L, L), 0)
    col = lax.broadcasted_iota(jnp.int32, (L, L), 1)
    causal = col <= row
    ifc = ifc_ref[...] + bic_ref[...]
    ifr = ifr_ref[...] + bir_ref[...]
    ones = jnp.ones((L, LANES), BF16)

    for h in range(ML_HEADS):
        sl = slice(h * ML_HEAD_DIM, (h + 1) * ML_HEAD_DIM)
        i_col = ifc[:, h:h + 1]
        f_col = ifc[:, ML_HEADS + h:ML_HEADS + h + 1]
        i_row = ifr[h:h + 1, :]
        f_row = ifr[ML_HEADS + h:ML_HEADS + h + 1, :]
        lf_col = jnp.minimum(f_col, 0.0) - jnp.log(1.0 + jnp.exp(-jnp.abs(f_col)))
        lf_row = jnp.minimum(f_row, 0.0) - jnp.log(1.0 + jnp.exp(-jnp.abs(f_row)))
        b_col = jnp.sum(jnp.where(causal, lf_row, 0.0), axis=1, keepdims=True)
        b_row = jnp.sum(jnp.where(row <= col, lf_col, 0.0), axis=0, keepdims=True)
        b_last = jnp.sum(lf_row, axis=1, keepdims=True)
        m_st = m_scr[h, 0:1, 0:1]

        log_d = jnp.where(causal, b_col - b_row + i_row, -jnp.inf)
        m_inter = b_col + m_st
        m_t = jnp.maximum(m_inter, jnp.max(log_d, axis=1, keepdims=True))
        q_h = qc[:, sl].astype(BF16)
        k_h = kc[:, sl]
        v_ext = jnp.concatenate([v_ref[:, sl], ones], axis=1)
        s = _dot_nt(q_h, k_h.astype(BF16)) * jnp.exp(log_d - m_t)
        w_inter = jnp.exp(m_inter - m_t)
        ct = ct_scr[h]
        num = _dot(s.astype(BF16), v_ext) + w_inter * _dot(q_h, ct.astype(BF16))
        den = num[:, ML_HEAD_DIM:]
        hh = num[:, :ML_HEAD_DIM] / jnp.maximum(jnp.abs(den), jnp.exp(-m_t))

        log_g = b_last - b_col + i_col
        m_new = jnp.maximum(b_last + m_st, jnp.max(log_g, axis=0, keepdims=True))
        decay = jnp.exp(b_last + m_st - m_new)
        wk = jnp.exp(log_g - m_new)
        ct_scr[h] = decay * ct + _dot_tn((wk * k_h).astype(BF16), v_ext)
        m_scr[h] = jnp.broadcast_to(m_new, (SUBLANES, LANES))

        hn = hh * lax.rsqrt(jnp.mean(hh * hh, axis=-1, keepdims=True) + EPS) * gout_ref[:, sl]
        y_ref[:, sl] = (_sigmoid(o_ref[:, sl].astype(F32)) * hn).astype(BF16)


def _mlstm(proj, ifc, ifr, conv_w, conv_b, bias_col, bias_row, g_out, bsz, seq, cols):
    qcol, kcol, vcol, ocol = cols
    L = ML_CHUNK
    W = ML_HEADS * ML_HEAD_DIM
    nc = seq // L

    def colspec(cb):
        return pl.BlockSpec((L, W), lambda b, c: (b * nc + c, cb))

    def const(shape):
        return pl.BlockSpec(shape, lambda b, c: (0,) * len(shape))

    return pl.pallas_call(
        _mlstm_kernel,
        grid=(bsz, nc),
        in_specs=[
            colspec(qcol), colspec(kcol), colspec(vcol), colspec(ocol),
            pl.BlockSpec((L, LANES), lambda b, c: (b * nc + c, 0)),
            pl.BlockSpec((SUBLANES, L), lambda b, c: (0, b * nc + c)),
            const((ML_CONV, 2 * W)), const((1, 2 * W)),
            const((1, LANES)), const((SUBLANES, 1)), const((1, W)),
        ],
        out_specs=pl.BlockSpec((L, W), lambda b, c: (b * nc + c, 0)),
        out_shape=jax.ShapeDtypeStruct((bsz * seq, W), BF16),
        scratch_shapes=[
            pltpu.VMEM((L + 2 * SUBLANES, W), F32),
            pltpu.VMEM((L + 2 * SUBLANES, W), F32),
            pltpu.VMEM((ML_HEADS, ML_HEAD_DIM, ML_HEAD_DIM + LANES), F32),
            pltpu.VMEM((ML_HEADS, SUBLANES, LANES), F32),
        ],
        compiler_params=_cparams(("arbitrary", "arbitrary")),
        name="mlstm",
    )(proj, proj, proj, proj, ifc, ifr, conv_w, conv_b, bias_col, bias_row, g_out)


def _memkv_kernel(mem_ref, g_ref, w_ref, gk_ref, k_ref, v_ref):
    x = mem_ref[...]
    ms = jnp.mean(x * x, axis=-1, keepdims=True)
    mn = (x * lax.rsqrt(ms + EPS) * g_ref[...]).astype(BF16)
    kv = _dot(mn, w_ref[...])
    W = XA_HEADS * XA_HEAD_DIM
    for h in range(XA_HEADS):
        sl = slice(h * XA_HEAD_DIM, (h + 1) * XA_HEAD_DIM)
        kh = kv[:, sl]
        kn = kh * lax.rsqrt(jnp.mean(kh * kh, axis=-1, keepdims=True) + EPS) * gk_ref[...]
        k_ref[:, sl] = kn.astype(BF16)
    v_ref[...] = kv[:, W:].astype(BF16)


def _memkv(mem2, g_mem, w_kv, g_k, bsz, n_mem):
    d = mem2.shape[1]
    W = XA_HEADS * XA_HEAD_DIM
    return pl.pallas_call(
        _memkv_kernel,
        grid=(bsz,),
        in_specs=[
            pl.BlockSpec((n_mem, d), lambda b: (b, 0)),
            pl.BlockSpec((1, d), lambda b: (0, 0)),
            pl.BlockSpec((d, 2 * W), lambda b: (0, 0)),
            pl.BlockSpec((1, XA_HEAD_DIM), lambda b: (0, 0)),
        ],
        out_specs=[pl.BlockSpec((n_mem, W), lambda b: (b, 0)),
                   pl.BlockSpec((n_mem, W), lambda b: (b, 0))],
        out_shape=[jax.ShapeDtypeStruct((bsz * n_mem, W), BF16)] * 2,
        compiler_params=_cparams(("arbitrary",)),
        name="memkv",
    )(mem2, g_mem, w_kv, g_k)


def _xattn_kernel(q_ref, k_ref, v_ref, gq_ref, y_ref):
    scale = 1.0 / math.sqrt(XA_HEAD_DIM)
    for h in range(XA_HEADS):
        sl = slice(h * XA_HEAD_DIM, (h + 1) * XA_HEAD_DIM)
        qh = q_ref[:, sl].astype(F32)
        qn = qh * lax.rsqrt(jnp.mean(qh * qh, axis=-1, keepdims=True) + EPS) * gq_ref[...]
        s = _dot_nt(qn.astype(BF16), k_ref[:, sl]) * scale
        p = jnp.exp(s - jnp.max(s, axis=-1, keepdims=True))
        p = p / jnp.sum(p, axis=-1, keepdims=True)
        y_ref[:, sl] = _dot(p.astype(BF16), v_ref[:, sl]).astype(BF16)


def _xattn(proj, kn, vv, g_q, bsz, seq, n_mem, qcol, tm):
    W = XA_HEADS * XA_HEAD_DIM
    nt = seq // tm
    return pl.pallas_call(
        _xattn_kernel,
        grid=(bsz, nt),
        in_specs=[
            pl.BlockSpec((tm, W), lambda b, i: (b * nt + i, qcol)),
            pl.BlockSpec((n_mem, W), lambda b, i: (b, 0)),
            pl.BlockSpec((n_mem, W), lambda b, i: (b, 0)),
            pl.BlockSpec((1, XA_HEAD_DIM), lambda b, i: (0, 0)),
        ],
        out_specs=pl.BlockSpec((tm, W), lambda b, i: (b * nt + i, 0)),
        out_shape=jax.ShapeDtypeStruct((bsz * seq, W), BF16),
        compiler_params=_cparams(("arbitrary", "arbitrary")),
        name="xattn",
    )(proj, kn, vv, g_q)


def _merge_kernel(x_ref, ysb_ref, yml_ref, yxa_ref, g0_ref, g1_ref, g2_ref,
                  wsb_ref, wml_ref, wxa_ref, wo_ref, gf_ref, x1_ref, h2_ref):
    merged = _sigmoid(g0_ref[...].astype(F32)) * _dot(ysb_ref[...], wsb_ref[...])
    merged = merged + _sigmoid(g1_ref[...].astype(F32)) * _dot(yml_ref[...], wml_ref[...])
    merged = merged + _sigmoid(g2_ref[...].astype(F32)) * _dot(yxa_ref[...], wxa_ref[...])
    x1 = x_ref[...] + _dot(merged.astype(BF16), wo_ref[...])
    x1_ref[...] = x1
    ms = jnp.mean(x1 * x1, axis=-1, keepdims=True)
    h2_ref[...] = (x1 * lax.rsqrt(ms + EPS) * gf_ref[...]).astype(BF16)


def _merge(x2, ysb, yml, yxa, proj, gcol, w_sb, w_ml, w_xa, w_o, g_ffn, tm):
    t, d = x2.shape
    wb = ysb.shape[1]

    def rows(width, cb=0):
        return pl.BlockSpec((tm, width), lambda i: (i, cb))

    def const(shape):
        return pl.BlockSpec(shape, lambda i: (0, 0))

    return pl.pallas_call(
        _merge_kernel,
        grid=(t // tm,),
        in_specs=[
            rows(d), rows(wb), rows(wb), rows(wb),
            rows(d, gcol), rows(d, gcol + 1), rows(d, gcol + 2),
            const((wb, d)), const((wb, d)), const((wb, d)), const((d, d)), const((1, d)),
        ],
        out_specs=[rows(d), rows(d)],
        out_shape=[jax.ShapeDtypeStruct((t, d), F32), jax.ShapeDtypeStruct((t, d), BF16)],
        compiler_params=_cparams(("arbitrary",)),
        name="merge",
    )(x2, ysb, yml, yxa, proj, proj, proj, w_sb, w_ml, w_xa, w_o, g_ffn)


def _ffn_kernel(x1_ref, h2_ref, wv_ref, wg_ref, cwv_ref, cwg_ref, cbv_ref, cbg_ref, wd_ref,
                o_ref, vbuf, gbuf, vtail, gtail, acc, *, tm, tiles_per_seq):
    i = pl.program_id(0)
    j = pl.program_id(1)
    nj = pl.num_programs(1)

    @pl.when(i % tiles_per_seq == 0)
    def _():
        vtail[j] = jnp.zeros(vtail.shape[1:], F32)
        gtail[j] = jnp.zeros(gtail.shape[1:], F32)

    h2 = h2_ref[...]

    def conv(buf, tail, w_up_ref, cw_ref, cb_ref):
        buf[0:SUBLANES, :] = tail[j]
        buf[SUBLANES:SUBLANES + tm, :] = _dot(h2, w_up_ref[...])
        tail[j] = buf[tm:tm + SUBLANES, :]
        cw = cw_ref[...]
        y = cb_ref[...]
        for t in range(FF_CONV):
            off = SUBLANES - (FF_CONV - 1) + t
            y = y + cw[t:t + 1, :] * buf[off:off + tm, :]
        return y

    uv = conv(vbuf, vtail, wv_ref, cwv_ref, cbv_ref)
    ug = conv(gbuf, gtail, wg_ref, cwg_ref, cbg_ref)
    act = (ug * _sigmoid(ug) * uv).astype(BF16)
    part = _dot(act, wd_ref[...])

    @pl.when(j == 0)
    def _():
        acc[...] = x1_ref[...] + part

    @pl.when(j > 0)
    def _():
        acc[...] += part

    @pl.when(j == nj - 1)
    def _():
        o_ref[...] = acc[...]


def _ffn(x1, h2, w_up, conv_w, conv_b, w_down, seq, tm, tf):
    t, d = x1.shape
    dff = w_down.shape[0]
    nj = dff // tf
    return pl.pallas_call(
        functools.partial(_ffn_kernel, tm=tm, tiles_per_seq=seq // tm),
        grid=(t // tm, nj),
        in_specs=[
            pl.BlockSpec((tm, d), lambda i, j: (i, 0)),
            pl.BlockSpec((tm, d), lambda i, j: (i, 0)),
            pl.BlockSpec((d, tf), lambda i, j: (0, j)),
            pl.BlockSpec((d, tf), lambda i, j: (0, nj + j)),
            pl.BlockSpec((FF_CONV, tf), lambda i, j: (0, j)),
            pl.BlockSpec((FF_CONV, tf), lambda i, j: (0, nj + j)),
            pl.BlockSpec((1, tf), lambda i, j: (0, j)),
            pl.BlockSpec((1, tf), lambda i, j: (0, nj + j)),
            pl.BlockSpec((tf, d), lambda i, j: (j, 0)),
        ],
        out_specs=pl.BlockSpec((tm, d), lambda i, j: (i, 0)),
        out_shape=jax.ShapeDtypeStruct((t, d), F32),
        scratch_shapes=[
            pltpu.VMEM((tm + SUBLANES, tf), F32),
            pltpu.VMEM((tm + SUBLANES, tf), F32),
            pltpu.VMEM((nj, SUBLANES, tf), F32),
            pltpu.VMEM((nj, SUBLANES, tf), F32),
            pltpu.VMEM((tm, d), F32),
        ],
        compiler_params=_cparams(("arbitrary", "arbitrary")),
        name="ffn",
    )(x1, h2, w_up, w_up, conv_w, conv_w, conv_b, conv_b, w_down)


def kernel(x, mem, g_mix, w_in, ml_conv_w, ml_conv_b, ml_b_i, ml_b_f, ml_g_out, g_mem,
           w_mem_kv, xa_g_q, xa_g_k, w_sb_out, w_ml_out, w_xa_out, w_o, g_ffn, w_up,
           ff_conv_w, ff_conv_b, w_down):
    bsz, seq, d = x.shape
    n_mem = mem.shape[1]
    depth = w_in.shape[0]
    sbw = SB_HEADS * SB_HEAD_DIM
    mlw = ML_HEADS * ML_HEAD_DIM
    xaw = XA_HEADS * XA_HEAD_DIM
    n_wide = 3 * sbw + 4 * mlw
    if_lo, if_hi = n_wide, n_wide + 2 * ML_HEADS
    sb_scale = math.log2(math.e) / math.sqrt(SB_HEAD_DIM)

    x2 = x.reshape(bsz * seq, d)
    mem2 = mem.reshape(bsz * n_mem, d)
    for l in range(depth):
        wl = w_in[l]
        w_main = jnp.concatenate(
            [wl[:, :sbw] * (-sb_scale), wl[:, sbw:n_wide], wl[:, if_hi:]], axis=1).astype(BF16)
        w_if = wl[:, if_lo:if_hi]
        w_ifc = jnp.pad(w_if, ((0, 0), (0, LANES - 2 * ML_HEADS))).astype(BF16)
        w_ifr = w_if.T.astype(BF16)
        proj, ifc, ifr = _inproj(x2, g_mix[l][None, :], w_main, w_ifc, w_ifr, tm=512, tn=1024)

        ysb = _sb_attention(proj, bsz, seq, qcol=0, kcol=sbw // LANES, vcol=2 * sbw // LANES, tq=512)

        bias = jnp.concatenate([ml_b_i[l], ml_b_f[l]])
        bias_col = jnp.pad(bias, (0, LANES - 2 * ML_HEADS))[None, :]
        bias_row = bias[:, None]
        ml0 = 3 * sbw // mlw
        yml = _mlstm(proj, ifc, ifr, ml_conv_w[l], ml_conv_b[l][None, :], bias_col, bias_row,
                     ml_g_out[l].reshape(1, mlw), bsz, seq, cols=(ml0, ml0 + 1, ml0 + 2, ml0 + 3))

        kn, vv = _memkv(mem2, g_mem[l][None, :], w_mem_kv[l].astype(BF16), xa_g_k[l][None, :], bsz, n_mem)
        yxa = _xattn(proj, kn, vv, xa_g_q[l][None, :], bsz, seq, n_mem, qcol=n_wide // xaw, tm=512)

        x1, h2 = _merge(x2, ysb, yml, yxa, proj, (n_wide + xaw) // d,
                        w_sb_out[l].astype(BF16), w_ml_out[l].astype(BF16), w_xa_out[l].astype(BF16),
                        w_o[l].astype(BF16), g_ffn[l][None, :], tm=512)

        x2 = _ffn(x1, h2, w_up[l].astype(BF16), ff_conv_w[l], ff_conv_b[l][None, :],
                  w_down[l].astype(BF16), seq, tm=512, tf=256)
    return x2.reshape(bsz, seq, d)
```

```python
import functools
import math

import jax
import jax.numpy as jnp
from jax import lax
from jax.experimental import pallas as pl
from jax.experimental.pallas import tpu as pltpu

F32 = jnp.float32
BF16 = jnp.bfloat16

SB_HEADS = 8
SB_HEAD_DIM = 64
ML_HEADS = 4
ML_HEAD_DIM = 128
ML_CONV = 4
ML_CHUNK = 128
XA_HEADS = 4
XA_HEAD_DIM = 128
FF_CONV = 3
EPS = 1e-6
SB_UNROLL = 2
SB_DEAD_LOG2 = -150.0

LANES = 128
SUBLANES = 8
VMEM_LIMIT = 56 * 1024 * 1024


def _cparams(sem):
    return pltpu.CompilerParams(dimension_semantics=sem, vmem_limit_bytes=VMEM_LIMIT)


def _dot(a, b):
    return jnp.dot(a, b, preferred_element_type=F32)


def _dot_nt(a, b):
    return lax.dot_general(a, b, (((1,), (1,)), ((), ())), preferred_element_type=F32)


def _dot_tn(a, b):
    return lax.dot_general(a, b, (((0,), (0,)), ((), ())), preferred_element_type=F32)


def _sigmoid(x):
    return 1.0 / (1.0 + jnp.exp(-x))


def _inproj_kernel(x_ref, g_ref, w_ref, wif_ref, wift_ref, o_ref, ifc_ref, ifr_ref, h_scr, *, tn):
    j = pl.program_id(1)

    @pl.when(j == 0)
    def _():
        x = x_ref[...]
        ms = jnp.mean(x * x, axis=-1, keepdims=True)
        h = (x * lax.rsqrt(ms + EPS) * g_ref[...]).astype(BF16)
        h_scr[...] = h
        ifc_ref[...] = _dot(h, wif_ref[...])
        ifr_ref[...] = _dot_nt(wift_ref[...], h)

    o_ref[...] = _dot(h_scr[...], w_ref[:, pl.ds(pl.multiple_of(j * tn, LANES), tn)]).astype(BF16)


def _inproj(x2, g, w_main, w_if, w_ift, tm, tn):
    t, d = x2.shape
    n = w_main.shape[1]
    return pl.pallas_call(
        functools.partial(_inproj_kernel, tn=tn),
        grid=(t // tm, n // tn),
        in_specs=[
            pl.BlockSpec((tm, d), lambda i, j: (i, 0)),
            pl.BlockSpec((1, d), lambda i, j: (0, 0)),
            pl.BlockSpec((d, n), lambda i, j: (0, 0), pipeline_mode=pl.Buffered(1)),
            pl.BlockSpec((d, LANES), lambda i, j: (0, 0)),
            pl.BlockSpec((SUBLANES, d), lambda i, j: (0, 0)),
        ],
        out_specs=[
            pl.BlockSpec((tm, tn), lambda i, j: (i, j)),
            pl.BlockSpec((tm, LANES), lambda i, j: (i, 0)),
            pl.BlockSpec((SUBLANES, tm), lambda i, j: (0, i)),
        ],
        out_shape=[
            jax.ShapeDtypeStruct((t, n), BF16),
            jax.ShapeDtypeStruct((t, LANES), F32),
            jax.ShapeDtypeStruct((SUBLANES, t), F32),
        ],
        scratch_shapes=[pltpu.VMEM((tm, d), BF16)],
        compiler_params=_cparams(("arbitrary", "arbitrary")),
        name="inproj",
    )(x2, g, w_main, w_if, w_ift)


def _sb_kernel(q_ref, k_ref, v_ref, o_ref, kk_scr, vv_scr, acc_scr, carry_scr, *, tq):
    i = pl.program_id(2)
    tk = LANES
    nsub = tq // tk
    seq = k_ref.shape[0]

    @pl.when(i == 0)
    def _():
        lane = lax.broadcasted_iota(jnp.int32, (seq, LANES), 1)
        for src, dst in ((k_ref, kk_scr), (v_ref, vv_scr)):
            a = src[...]
            zero = jnp.zeros_like(a)
            dst[:, 0:tk, :] = jnp.where(lane < SB_HEAD_DIM, a, zero).reshape(seq // tk, tk, LANES)
            dst[:, tk:2 * tk, :] = jnp.where(lane >= SB_HEAD_DIM, a, zero).reshape(seq // tk, tk, LANES)

    r4 = lax.broadcasted_iota(jnp.int32, (4 * tk, 2 * tk), 0)
    c4 = lax.broadcasted_iota(jnp.int32, (4 * tk, 2 * tk), 1)
    same_head = ((r4 // tk) % 2) == (c4 // tk)
    suffix = jnp.where(same_head & ((r4 % tk) > (c4 % tk)), 1.0, 0.0).astype(BF16)

    acc_scr[...] = jnp.zeros_like(acc_scr)
    carry_scr[...] = jnp.zeros_like(carry_scr)

    def step(kb_lo, nblk, r0, diag_c):
        n = tq - r0
        q = q_ref[r0:tq, :]
        kk = kk_scr[pl.ds(kb_lo, nblk)].reshape(nblk * 2 * tk, LANES)
        y = _dot_nt(q, kk)
        neg_abs = lax.bitcast_convert_type(
            lax.bitcast_convert_type(y, jnp.uint32) | jnp.uint32(0x80000000), F32)
        log_fail = jnp.minimum(y, 0.0) - jnp.log2(1.0 + jnp.exp2(neg_abs))
        if diag_c is not None:
            rr = lax.broadcasted_iota(jnp.int32, (n, nblk * 2 * tk), 0) + r0
            cc = lax.broadcasted_iota(jnp.int32, (n, nblk * 2 * tk), 1)
            cc = cc % tk + (cc // (2 * tk) + diag_c) * tk
            strict = cc < rr
            log_fail = jnp.where(strict, log_fail, 0.0)
        log_hit = log_fail - y
        carry = carry_scr[r0:tq, :]
        ws = [None] * nblk
        for b in reversed(range(nblk)):
            sl = slice(b * 2 * tk, (b + 1) * 2 * tk)
            lf = log_fail[:, sl]
            hi = lf.astype(BF16)
            lo = (lf - hi.astype(F32)).astype(BF16)
            suf = _dot(jnp.concatenate([hi, lo], axis=1), suffix)
            w = jnp.exp2(log_hit[:, sl] + (suf + carry))
            if diag_c is not None:
                w = jnp.where(strict[:, sl], w, 0.0)
            ws[b] = w.astype(BF16)
            tot0 = jnp.sum(lf[:, :tk], axis=1, keepdims=True)
            tot1 = jnp.sum(lf[:, tk:], axis=1, keepdims=True)
            carry = carry + jnp.concatenate(
                [jnp.broadcast_to(tot0, (n, tk)), jnp.broadcast_to(tot1, (n, tk))], axis=1)
        vv = vv_scr[pl.ds(kb_lo, nblk)].reshape(nblk * 2 * tk, LANES)
        acc_scr[r0:tq, :] += _dot(jnp.concatenate(ws, axis=1), vv)
        carry_scr[r0:tq, :] = carry

    for c in reversed(range(0, nsub, SB_UNROLL)):
        step(i * nsub + c, SB_UNROLL, c * tk, c)

    n_steps = i * nsub // SB_UNROLL

    def cond(state):
        it, live = state
        return jnp.logical_and(it < n_steps, live)

    def body(state):
        it, _ = state
        step(i * nsub - SB_UNROLL * (it + 1), SB_UNROLL, 0, None)
        return it + 1, jnp.max(carry_scr[...]) >= SB_DEAD_LOG2

    lax.while_loop(cond, body, (jnp.int32(0), jnp.bool_(True)))
    o_ref[...] = acc_scr[...].astype(BF16)


def _sb_attention(proj, bsz, seq, qcol, kcol, vcol, tq):
    npair = SB_HEADS * SB_HEAD_DIM // LANES
    nq = seq // tq
    return pl.pallas_call(
        functools.partial(_sb_kernel, tq=tq),
        grid=(bsz, npair, nq),
        in_specs=[
            pl.BlockSpec((tq, LANES), lambda b, p, i: (b * nq + i, qcol + p)),
            pl.BlockSpec((seq, LANES), lambda b, p, i: (b, kcol + p)),
            pl.BlockSpec((seq, LANES), lambda b, p, i: (b, vcol + p)),
        ],
        out_specs=pl.BlockSpec((tq, LANES), lambda b, p, i: (b * nq + i, p)),
        out_shape=jax.ShapeDtypeStruct((bsz * seq, npair * LANES), BF16),
        scratch_shapes=[
            pltpu.VMEM((seq // LANES, 2 * LANES, LANES), BF16),
            pltpu.VMEM((seq // LANES, 2 * LANES, LANES), BF16),
            pltpu.VMEM((tq, LANES), F32),
            pltpu.VMEM((tq, 2 * LANES), F32),
        ],
        compiler_params=_cparams(("arbitrary", "arbitrary", "arbitrary")),
        name="sb_attn",
    )(proj, proj, proj)


def _mlstm_kernel(q_ref, k_ref, v_ref, o_ref, ifc_ref, ifr_ref, cw_ref, cb_ref,
                  bic_ref, bir_ref, gout_ref, y_ref,
                  qbuf, kbuf, ct_scr, m_scr):
    c = pl.program_id(1)
    L = ML_CHUNK
    W = ML_HEADS * ML_HEAD_DIM

    @pl.when(c == 0)
    def _():
        qbuf[0:SUBLANES, :] = jnp.zeros((SUBLANES, W), F32)
        kbuf[0:SUBLANES, :] = jnp.zeros((SUBLANES, W), F32)
        ct_scr[...] = jnp.zeros_like(ct_scr)
        m_scr[...] = jnp.zeros_like(m_scr)

    qbuf[SUBLANES:SUBLANES + L, :] = q_ref[...].astype(F32)
    kbuf[SUBLANES:SUBLANES + L, :] = k_ref[...].astype(F32)

    def conv_silu(buf, w, b):
        y = b
        for j in range(ML_CONV):
            off = SUBLANES - (ML_CONV - 1) + j
            y = y + w[j:j + 1, :] * buf[off:off + L, :]
        return y * _sigmoid(y)

    cw = cw_ref[...]
    cb = cb_ref[...]
    qc = conv_silu(qbuf, cw[:, :W], cb[:, :W])
    kc = conv_silu(kbuf, cw[:, W:], cb[:, W:]) * (1.0 / math.sqrt(ML_HEAD_DIM))
    qbuf[0:SUBLANES, :] = qbuf[L:L + SUBLANES, :]
    kbuf[0:SUBLANES, :] = kbuf[L:L + SUBLANES, :]

    row = lax.broadcasted_iota(jnp.int32, (L, L), 0)
    col = lax.broadcasted_iota(jnp.int32, (L, L), 1)
    causal = col <= row
    ifc = ifc_ref[...] + bic_ref[...]
    ifr = ifr_ref[...] + bir_ref[...]
    ones = jnp.ones((L, LANES), BF16)

    for h in range(ML_HEADS):
        sl = slice(h * ML_HEAD_DIM, (h + 1) * ML_HEAD_DIM)
        i_col = ifc[:, h:h + 1]
        f_col = ifc[:, ML_HEADS + h:ML_HEADS + h + 1]
        i_row = ifr[h:h + 1, :]
        f_row = ifr[ML_HEADS + h:ML_HEADS + h + 1, :]
        lf_col = jnp.minimum(f_col, 0.0) - jnp.log(1.0 + jnp.exp(-jnp.abs(f_col)))
        lf_row = jnp.minimum(f_row, 0.0) - jnp.log(1.0 + jnp.exp(-jnp.abs(f_row)))
        b_col = jnp.sum(jnp.where(causal, lf_row, 0.0), axis=1, keepdims=True)
        b_row = jnp.sum(jnp.where(row <= col, lf_col, 0.0), axis=0, keepdims=True)
        b_last = jnp.sum(lf_row, axis=1, keepdims=True)
        m_st = m_scr[h, 0:1, 0:1]

        log_d = jnp.where(causal, b_col - b_row + i_row, -jnp.inf)
        m_inter = b_col + m_st
        m_t = jnp.maximum(m_inter, jnp.max(log_d, axis=1, keepdims=True))
        q_h = qc[:, sl].astype(BF16)
        k_h = kc[:, sl]
        v_ext = jnp.concatenate([v_ref[:, sl], ones], axis=1)
        s = _dot_nt(q_h, k_h.astype(BF16)) * jnp.exp(log_d - m_t)
        w_inter = jnp.exp(m_inter - m_t)
        ct = ct_scr[h]
        num = _dot(s.astype(BF16), v_ext) + w_inter * _dot(q_h, ct.astype(BF16))
        den = num[:, ML_HEAD_DIM:]
        hh = num[:, :ML_HEAD_DIM] / jnp.maximum(jnp.abs(den), jnp.exp(-m_t))

        log_g = b_last - b_col + i_col
        m_new = jnp.maximum(b_last + m_st, jnp.max(log_g, axis=0, keepdims=True))
        decay = jnp.exp(b_last + m_st - m_new)
        wk = jnp.exp(log_g - m_new)
        ct_scr[h] = decay * ct + _dot_tn((wk * k_h).astype(BF16), v_ext)
        m_scr[h] = jnp.broadcast_to(m_new, (SUBLANES, LANES))

        hn = hh * lax.rsqrt(jnp.mean(hh * hh, axis=-1, keepdims=True) + EPS) * gout_ref[:, sl]
        y_ref[:, sl] = (_sigmoid(o_ref[:, sl].astype(F32)) * hn).astype(BF16)


def _mlstm(proj, ifc, ifr, conv_w, conv_b, bias_col, bias_row, g_out, bsz, seq, cols):
    qcol, kcol, vcol, ocol = cols
    L = ML_CHUNK
    W = ML_HEADS * ML_HEAD_DIM
    nc = seq // L

    def colspec(cb):
        return pl.BlockSpec((L, W), lambda b, c: (b * nc + c, cb))

    def const(shape):
        return pl.BlockSpec(shape, lambda b, c: (0,) * len(shape))

    return pl.pallas_call(
        _mlstm_kernel,
        grid=(bsz, nc),
        in_specs=[
            colspec(qcol), colspec(kcol), colspec(vcol), colspec(ocol),
            pl.BlockSpec((L, LANES), lambda b, c: (b * nc + c, 0)),
            pl.BlockSpec((SUBLANES, L), lambda b, c: (0, b * nc + c)),
            const((ML_CONV, 2 * W)), const((1, 2 * W)),
            const((1, LANES)), const((SUBLANES, 1)), const((1, W)),
        ],
        out_specs=pl.BlockSpec((L, W), lambda b, c: (b * nc + c, 0)),
        out_shape=jax.ShapeDtypeStruct((bsz * seq, W), BF16),
        scratch_shapes=[
            pltpu.VMEM((L + 2 * SUBLANES, W), F32),
            pltpu.VMEM((L + 2 * SUBLANES, W), F32),
            pltpu.VMEM((ML_HEADS, ML_HEAD_DIM, ML_HEAD_DIM + LANES), F32),
            pltpu.VMEM((ML_HEADS, SUBLANES, LANES), F32),
        ],
        compiler_params=_cparams(("arbitrary", "arbitrary")),
        name="mlstm",
    )(proj, proj, proj, proj, ifc, ifr, conv_w, conv_b, bias_col, bias_row, g_out)


def _memkv_kernel(mem_ref, g_ref, w_ref, gk_ref, k_ref, v_ref):
    x = mem_ref[...]
    ms = jnp.mean(x * x, axis=-1, keepdims=True)
    mn = (x * lax.rsqrt(ms + EPS) * g_ref[...]).astype(BF16)
    kv = _dot(mn, w_ref[...])
    W = XA_HEADS * XA_HEAD_DIM
    for h in range(XA_HEADS):
        sl = slice(h * XA_HEAD_DIM, (h + 1) * XA_HEAD_DIM)
        kh = kv[:, sl]
        kn = kh * lax.rsqrt(jnp.mean(kh * kh, axis=-1, keepdims=True) + EPS) * gk_ref[...]
        k_ref[:, sl] = kn.astype(BF16)
    v_ref[...] = kv[:, W:].astype(BF16)


def _memkv(mem2, g_mem, w_kv, g_k, bsz, n_mem):
    d = mem2.shape[1]
    W = XA_HEADS * XA_HEAD_DIM
    return pl.pallas_call(
        _memkv_kernel,
        grid=(bsz,),
        in_specs=[
            pl.BlockSpec((n_mem, d), lambda b: (b, 0)),
            pl.BlockSpec((1, d), lambda b: (0, 0)),
            pl.BlockSpec((d, 2 * W), lambda b: (0, 0)),
            pl.BlockSpec((1, XA_HEAD_DIM), lambda b: (0, 0)),
        ],
        out_specs=[pl.BlockSpec((n_mem, W), lambda b: (b, 0)),
                   pl.BlockSpec((n_mem, W), lambda b: (b, 0))],
        out_shape=[jax.ShapeDtypeStruct((bsz * n_mem, W), BF16)] * 2,
        compiler_params=_cparams(("arbitrary",)),
        name="memkv",
    )(mem2, g_mem, w_kv, g_k)


def _xattn_kernel(q_ref, k_ref, v_ref, gq_ref, y_ref):
    scale = 1.0 / math.sqrt(XA_HEAD_DIM)
    for h in range(XA_HEADS):
        sl = slice(h * XA_HEAD_DIM, (h + 1) * XA_HEAD_DIM)
        qh = q_ref[:, sl].astype(F32)
        qn = qh * lax.rsqrt(jnp.mean(qh * qh, axis=-1, keepdims=True) + EPS) * gq_ref[...]
        s = _dot_nt(qn.astype(BF16), k_ref[:, sl]) * scale
        p = jnp.exp(s - jnp.max(s, axis=-1, keepdims=True))
        p = p / jnp.sum(p, axis=-1, keepdims=True)
        y_ref[:, sl] = _dot(p.astype(BF16), v_ref[:, sl]).astype(BF16)


def _xattn(proj, kn, vv, g_q, bsz, seq, n_mem, qcol, tm):
    W = XA_HEADS * XA_HEAD_DIM
    nt = seq // tm
    return pl.pallas_call(
        _xattn_kernel,
        grid=(bsz, nt),
        in_specs=[
            pl.BlockSpec((tm, W), lambda b, i: (b * nt + i, qcol)),
            pl.BlockSpec((n_mem, W), lambda b, i: (b, 0)),
            pl.BlockSpec((n_mem, W), lambda b, i: (b, 0)),
            pl.BlockSpec((1, XA_HEAD_DIM), lambda b, i: (0, 0)),
        ],
        out_specs=pl.BlockSpec((tm, W), lambda b, i: (b * nt + i, 0)),
        out_shape=jax.ShapeDtypeStruct((bsz * seq, W), BF16),
        compiler_params=_cparams(("arbitrary", "arbitrary")),
        name="xattn",
    )(proj, kn, vv, g_q)


def _merge_kernel(x_ref, ysb_ref, yml_ref, yxa_ref, gm_ref, wg_ref,
                  wsb_ref, wml_ref, wxa_ref, wo_ref, gf_ref, x1_ref, h2_ref):
    x = x_ref[...]
    d = x.shape[1]
    h = (x * lax.rsqrt(jnp.mean(x * x, axis=-1, keepdims=True) + EPS) * gm_ref[...]).astype(BF16)
    merged = None
    for br, (y_ref, w_ref) in enumerate(((ysb_ref, wsb_ref), (yml_ref, wml_ref), (yxa_ref, wxa_ref))):
        gate = _sigmoid(_dot(h, wg_ref[:, br * d:(br + 1) * d]))
        term = gate * _dot(y_ref[...], w_ref[...])
        merged = term if merged is None else merged + term
    x1 = x + _dot(merged.astype(BF16), wo_ref[...])
    x1_ref[...] = x1
    ms = jnp.mean(x1 * x1, axis=-1, keepdims=True)
    h2_ref[...] = (x1 * lax.rsqrt(ms + EPS) * gf_ref[...]).astype(BF16)


def _merge(x2, ysb, yml, yxa, g_mix, w_gate, w_sb, w_ml, w_xa, w_o, g_ffn, tm):
    t, d = x2.shape
    wb = ysb.shape[1]

    def rows(width):
        return pl.BlockSpec((tm, width), lambda i: (i, 0))

    def const(shape):
        return pl.BlockSpec(shape, lambda i: (0, 0), pipeline_mode=pl.Buffered(1))

    return pl.pallas_call(
        _merge_kernel,
        grid=(t // tm,),
        in_specs=[
            rows(d), rows(wb), rows(wb), rows(wb),
            const((1, d)), const((d, 3 * d)),
            const((wb, d)), const((wb, d)), const((wb, d)), const((d, d)), const((1, d)),
        ],
        out_specs=[rows(d), rows(d)],
        out_shape=[jax.ShapeDtypeStruct((t, d), F32), jax.ShapeDtypeStruct((t, d), BF16)],
        compiler_params=_cparams(("arbitrary",)),
        name="merge",
    )(x2, ysb, yml, yxa, g_mix, w_gate, w_sb, w_ml, w_xa, w_o, g_ffn)


def _ffn_kernel(x1_ref, h2_ref, wup_ref, cw_ref, cb_ref, wd_ref, o_ref, tail_scr, act_scr,
                *, tm, tf, tiles_per_seq):
    i = pl.program_id(0)
    j = pl.program_id(1)
    first = (i % tiles_per_seq) == 0
    h2 = h2_ref[...]

    for co in range(0, tf, LANES):
        col = pl.multiple_of(2 * (j * tf + co), 2 * LANES)
        u = _dot(h2, wup_ref[:, pl.ds(col, 2 * LANES)])
        tail = tail_scr[j, :, 2 * co:2 * co + 2 * LANES]
        tail = jnp.where(first, jnp.zeros_like(tail), tail)
        tail_scr[j, :, 2 * co:2 * co + 2 * LANES] = u[tm - SUBLANES:tm, :]
        head = jnp.concatenate([tail, u[0:SUBLANES, :]], axis=0)
        cw = cw_ref[:, pl.ds(col, 2 * LANES)]
        y = cb_ref[:, pl.ds(col, 2 * LANES)] + cw[FF_CONV - 1:FF_CONV, :] * u
        for back in range(1, FF_CONV):
            prev = jnp.concatenate(
                [pltpu.roll(head, back, axis=0)[SUBLANES:, :], pltpu.roll(u, back, axis=0)[SUBLANES:, :]],
                axis=0)
            y = y + cw[FF_CONV - 1 - back:FF_CONV - back, :] * prev
        uv, ug = y[:, :LANES], y[:, LANES:]
        act_scr[:, co:co + LANES] = (ug * _sigmoid(ug) * uv).astype(BF16)

    part = _dot(act_scr[...], wd_ref[pl.ds(pl.multiple_of(j * tf, LANES), tf), :])

    @pl.when(j == 0)
    def _():
        o_ref[...] = x1_ref[...] + part

    @pl.when(j > 0)
    def _():
        o_ref[...] += part


def _interleave_val_gate(a):
    dff = a.shape[-1] // 2
    parts = a.reshape(a.shape[:-1] + (2, dff // LANES, LANES))
    return jnp.swapaxes(parts, -3, -2).reshape(a.shape)


def _ffn(x1, h2, w_up, conv_w, conv_b, w_down, seq, tm, tf):
    t, d = x1.shape
    dff = w_down.shape[0]
    nj = dff // tf

    def resident(shape):
        return pl.BlockSpec(shape, lambda i, j: (0, 0), pipeline_mode=pl.Buffered(1))

    return pl.pallas_call(
        functools.partial(_ffn_kernel, tm=tm, tf=tf, tiles_per_seq=seq // tm),
        grid=(t // tm, nj),
        in_specs=[
            pl.BlockSpec((tm, d), lambda i, j: (i, 0)),
            pl.BlockSpec((tm, d), lambda i, j: (i, 0)),
            resident((d, 2 * dff)),
            resident((FF_CONV, 2 * dff)),
            resident((1, 2 * dff)),
            resident((dff, d)),
        ],
        out_specs=pl.BlockSpec((tm, d), lambda i, j: (i, 0)),
        out_shape=jax.ShapeDtypeStruct((t, d), F32),
        scratch_shapes=[
            pltpu.VMEM((nj, SUBLANES, 2 * tf), F32),
            pltpu.VMEM((tm, tf), BF16),
        ],
        compiler_params=_cparams(("arbitrary", "arbitrary")),
        name="ffn",
    )(x1, h2, _interleave_val_gate(w_up), _interleave_val_gate(conv_w), _interleave_val_gate(conv_b), w_down)


def kernel(x, mem, g_mix, w_in, ml_conv_w, ml_conv_b, ml_b_i, ml_b_f, ml_g_out, g_mem,
           w_mem_kv, xa_g_q, xa_g_k, w_sb_out, w_ml_out, w_xa_out, w_o, g_ffn, w_up,
           ff_conv_w, ff_conv_b, w_down):
    bsz, seq, d = x.shape
    n_mem = mem.shape[1]
    depth = w_in.shape[0]
    sbw = SB_HEADS * SB_HEAD_DIM
    mlw = ML_HEADS * ML_HEAD_DIM
    xaw = XA_HEADS * XA_HEAD_DIM
    n_wide = 3 * sbw + 4 * mlw
    if_lo, if_hi = n_wide, n_wide + 2 * ML_HEADS
    sb_scale = math.log2(math.e) / math.sqrt(SB_HEAD_DIM)

    x2 = x.reshape(bsz * seq, d)
    mem2 = mem.reshape(bsz * n_mem, d)
    for l in range(depth):
        wl = w_in[l]
        w_main = jnp.concatenate(
            [wl[:, :sbw] * (-sb_scale), wl[:, sbw:n_wide], wl[:, if_hi:if_hi + xaw]], axis=1).astype(BF16)
        w_gate = wl[:, if_hi + xaw:].astype(BF16)
        w_if = wl[:, if_lo:if_hi]
        w_ifc = jnp.pad(w_if, ((0, 0), (0, LANES - 2 * ML_HEADS))).astype(BF16)
        w_ifr = w_if.T.astype(BF16)
        proj, ifc, ifr = _inproj(x2, g_mix[l][None, :], w_main, w_ifc, w_ifr, tm=512, tn=2048)

        ysb = _sb_attention(proj, bsz, seq, qcol=0, kcol=sbw // LANES, vcol=2 * sbw // LANES, tq=512)

        bias = jnp.concatenate([ml_b_i[l], ml_b_f[l]])
        bias_col = jnp.pad(bias, (0, LANES - 2 * ML_HEADS))[None, :]
        bias_row = bias[:, None]
        ml0 = 3 * sbw // mlw
        yml = _mlstm(proj, ifc, ifr, ml_conv_w[l], ml_conv_b[l][None, :], bias_col, bias_row,
                     ml_g_out[l].reshape(1, mlw), bsz, seq, cols=(ml0, ml0 + 1, ml0 + 2, ml0 + 3))

        kn, vv = _memkv(mem2, g_mem[l][None, :], w_mem_kv[l].astype(BF16), xa_g_k[l][None, :], bsz, n_mem)
        yxa = _xattn(proj, kn, vv, xa_g_q[l][None, :], bsz, seq, n_mem, qcol=n_wide // xaw, tm=512)

        x1, h2 = _merge(x2, ysb, yml, yxa, g_mix[l][None, :], w_gate,
                        w_sb_out[l].astype(BF16), w_ml_out[l].astype(BF16), w_xa_out[l].astype(BF16),
                        w_o[l].astype(BF16), g_ffn[l][None, :], tm=512)

        x2 = _ffn(x1, h2, w_up[l].astype(BF16), ff_conv_w[l], ff_conv_b[l][None, :],
                  w_down[l].astype(BF16), seq, tm=512, tf=1408)
    return x2.reshape(bsz, seq, d)
```

```python
import functools
import math

import jax
import jax.numpy as jnp
from jax import lax
from jax.experimental import pallas as pl
from jax.experimental.pallas import tpu as pltpu

F32 = jnp.float32
BF16 = jnp.bfloat16

SB_HEADS = 8
SB_HEAD_DIM = 64
ML_HEADS = 4
ML_HEAD_DIM = 128
ML_CONV = 4
ML_CHUNK = 128
XA_HEADS = 4
XA_HEAD_DIM = 128
FF_CONV = 3
EPS = 1e-6
SB_UNROLL = 2
SB_DEAD_LOG2 = -150.0
ML_BATCH = 1
FFN_ROW_SPLIT = 2

LANES = 128
SUBLANES = 8
VMEM_LIMIT = 56 * 1024 * 1024


def _cparams(sem):
    return pltpu.CompilerParams(dimension_semantics=sem, vmem_limit_bytes=VMEM_LIMIT)


def _dot(a, b):
    return jnp.dot(a, b, preferred_element_type=F32)


def _dot_nt(a, b):
    return lax.dot_general(a, b, (((1,), (1,)), ((), ())), preferred_element_type=F32)


def _dot_tn(a, b):
    return lax.dot_general(a, b, (((0,), (0,)), ((), ())), preferred_element_type=F32)


def _sigmoid(x):
    return 1.0 / (1.0 + jnp.exp(-x))


def _inproj_kernel(x_ref, g_ref, w_ref, wif_ref, o_ref, ifc_ref, ifr_ref, h_scr, *, tn):
    j = pl.program_id(1)

    @pl.when(j == 0)
    def _():
        x = x_ref[...]
        ms = jnp.mean(x * x, axis=-1, keepdims=True)
        h = (x * lax.rsqrt(ms + EPS) * g_ref[...]).astype(BF16)
        h_scr[...] = h
        ifc = _dot(h, wif_ref[...])
        ifc_ref[...] = ifc
        ifr_ref[...] = ifc.T[0:SUBLANES, :]

    o_ref[...] = _dot(h_scr[...], w_ref[:, pl.ds(pl.multiple_of(j * tn, LANES), tn)]).astype(BF16)


def _inproj(x2, g, w_main, w_if, tm, tn):
    t, d = x2.shape
    n = w_main.shape[1]
    return pl.pallas_call(
        functools.partial(_inproj_kernel, tn=tn),
        grid=(t // tm, n // tn),
        in_specs=[
            pl.BlockSpec((tm, d), lambda i, j: (i, 0)),
            pl.BlockSpec((1, d), lambda i, j: (0, 0)),
            pl.BlockSpec((d, n), lambda i, j: (0, 0), pipeline_mode=pl.Buffered(1)),
            pl.BlockSpec((d, LANES), lambda i, j: (0, 0)),
        ],
        out_specs=[
            pl.BlockSpec((tm, tn), lambda i, j: (i, j)),
            pl.BlockSpec((tm, LANES), lambda i, j: (i, 0)),
            pl.BlockSpec((SUBLANES, tm), lambda i, j: (0, i)),
        ],
        out_shape=[
            jax.ShapeDtypeStruct((t, n), BF16),
            jax.ShapeDtypeStruct((t, LANES), F32),
            jax.ShapeDtypeStruct((SUBLANES, t), F32),
        ],
        scratch_shapes=[pltpu.VMEM((tm, d), BF16)],
        compiler_params=_cparams(("arbitrary", "arbitrary")),
        name="inproj",
    )(x2, g, w_main, w_if)


def _sb_kernel(q_ref, k_ref, v_ref, o_ref, kk_scr, vv_scr, acc_scr, carry_scr, *, tq):
    i = pl.program_id(2)
    tk = LANES
    nsub = tq // tk
    seq = k_ref.shape[0]

    @pl.when(i == 0)
    def _():
        lane = lax.broadcasted_iota(jnp.int32, (seq, LANES), 1)
        for src, dst in ((k_ref, kk_scr), (v_ref, vv_scr)):
            a = src[...]
            zero = jnp.zeros_like(a)
            dst[:, 0:tk, :] = jnp.where(lane < SB_HEAD_DIM, a, zero).reshape(seq // tk, tk, LANES)
            dst[:, tk:2 * tk, :] = jnp.where(lane >= SB_HEAD_DIM, a, zero).reshape(seq // tk, tk, LANES)

    r4 = lax.broadcasted_iota(jnp.int32, (4 * tk, 2 * tk), 0)
    c4 = lax.broadcasted_iota(jnp.int32, (4 * tk, 2 * tk), 1)
    same_head = ((r4 // tk) % 2) == (c4 // tk)
    suffix = jnp.where(same_head & ((r4 % tk) > (c4 % tk)), 1.0, 0.0).astype(BF16)

    acc_scr[...] = jnp.zeros_like(acc_scr)
    carry_scr[...] = jnp.zeros_like(carry_scr)

    def step(kb_lo, nblk, r0, diag_c):
        n = tq - r0
        q = q_ref[r0:tq, :]
        kk = kk_scr[pl.ds(kb_lo, nblk)].reshape(nblk * 2 * tk, LANES)
        y = _dot_nt(q, kk)
        neg_abs = lax.bitcast_convert_type(
            lax.bitcast_convert_type(y, jnp.uint32) | jnp.uint32(0x80000000), F32)
        log_fail = jnp.minimum(y, 0.0) - jnp.log2(1.0 + jnp.exp2(neg_abs))
        if diag_c is not None:
            rr = lax.broadcasted_iota(jnp.int32, (n, nblk * 2 * tk), 0) + r0
            cc = lax.broadcasted_iota(jnp.int32, (n, nblk * 2 * tk), 1)
            cc = cc % tk + (cc // (2 * tk) + diag_c) * tk
            strict = cc < rr
            log_fail = jnp.where(strict, log_fail, 0.0)
        log_hit = log_fail - y
        carry = carry_scr[r0:tq, :]
        ws = [None] * nblk
        for b in reversed(range(nblk)):
            sl = slice(b * 2 * tk, (b + 1) * 2 * tk)
            lf = log_fail[:, sl]
            hi = lf.astype(BF16)
            lo = (lf - hi.astype(F32)).astype(BF16)
            suf = _dot(jnp.concatenate([hi, lo], axis=1), suffix)
            w = jnp.exp2(log_hit[:, sl] + (suf + carry))
            if diag_c is not None:
                w = jnp.where(strict[:, sl], w, 0.0)
            ws[b] = w.astype(BF16)
            tot0 = jnp.sum(lf[:, :tk], axis=1, keepdims=True)
            tot1 = jnp.sum(lf[:, tk:], axis=1, keepdims=True)
            carry = carry + jnp.concatenate(
                [jnp.broadcast_to(tot0, (n, tk)), jnp.broadcast_to(tot1, (n, tk))], axis=1)
        vv = vv_scr[pl.ds(kb_lo, nblk)].reshape(nblk * 2 * tk, LANES)
        acc_scr[r0:tq, :] += _dot(jnp.concatenate(ws, axis=1), vv)
        carry_scr[r0:tq, :] = carry

    for c in reversed(range(0, nsub, SB_UNROLL)):
        step(i * nsub + c, SB_UNROLL, c * tk, c)

    n_steps = i * nsub // SB_UNROLL

    def cond(state):
        it, live = state
        return jnp.logical_and(it < n_steps, live)

    def body(state):
        it, _ = state
        step(i * nsub - SB_UNROLL * (it + 1), SB_UNROLL, 0, None)
        return it + 1, jnp.max(carry_scr[...]) >= SB_DEAD_LOG2

    lax.while_loop(cond, body, (jnp.int32(0), jnp.bool_(True)))
    o_ref[...] = acc_scr[...].astype(BF16)


def _sb_attention(proj, bsz, seq, qcol, kcol, vcol, tq):
    npair = SB_HEADS * SB_HEAD_DIM // LANES
    nq = seq // tq
    return pl.pallas_call(
        functools.partial(_sb_kernel, tq=tq),
        grid=(bsz, npair, nq),
        in_specs=[
            pl.BlockSpec((tq, LANES), lambda b, p, i: (b * nq + i, qcol + p)),
            pl.BlockSpec((seq, LANES), lambda b, p, i: (b, kcol + p)),
            pl.BlockSpec((seq, LANES), lambda b, p, i: (b, vcol + p)),
        ],
        out_specs=pl.BlockSpec((tq, LANES), lambda b, p, i: (b * nq + i, p)),
        out_shape=jax.ShapeDtypeStruct((bsz * seq, npair * LANES), BF16),
        scratch_shapes=[
            pltpu.VMEM((seq // LANES, 2 * LANES, LANES), BF16),
            pltpu.VMEM((seq // LANES, 2 * LANES, LANES), BF16),
            pltpu.VMEM((tq, LANES), F32),
            pltpu.VMEM((tq, 2 * LANES), F32),
        ],
        compiler_params=_cparams(("arbitrary", "arbitrary", "arbitrary")),
        name="sb_attn",
    )(proj, proj, proj)


def _mlstm_kernel(*refs, nb):
    q_ref, k_ref, v_ref, o_ref, ifc_ref = refs[:5]
    ifr_refs = refs[5:5 + nb]
    cw_ref, cb_ref, bic_ref, bir_ref, gout_ref, y_ref, qtail, ktail, ct_scr, m_scr = refs[5 + nb:]
    c = pl.program_id(1)
    L = ML_CHUNK
    W = ML_HEADS * ML_HEAD_DIM

    @pl.when(c == 0)
    def _():
        qtail[...] = jnp.zeros_like(qtail)
        ktail[...] = jnp.zeros_like(ktail)
        ct_scr[...] = jnp.zeros_like(ct_scr)
        m_scr[...] = jnp.zeros_like(m_scr)

    stores = []
    m_all = m_scr[...]
    ct_all = [ct_scr[st] for st in range(nb * ML_HEADS)]
    qtails = [qtail[r] for r in range(nb)]
    ktails = [ktail[r] for r in range(nb)]

    def conv_silu(x, tail, tail_ref, r, w, b):
        head = jnp.concatenate([tail, x[0:SUBLANES, :]], axis=0)
        stores.append((tail_ref, r, x[L - SUBLANES:L, :]))
        y = b + w[ML_CONV - 1:ML_CONV, :] * x
        for back in range(1, ML_CONV):
            prev = jnp.concatenate(
                [pltpu.roll(head, back, axis=0)[SUBLANES:, :], pltpu.roll(x, back, axis=0)[SUBLANES:, :]],
                axis=0)
            y = y + w[ML_CONV - 1 - back:ML_CONV - back, :] * prev
        return y * _sigmoid(y)

    cw = cw_ref[...]
    cb = cb_ref[...]
    row = lax.broadcasted_iota(jnp.int32, (L, L), 0)
    col = lax.broadcasted_iota(jnp.int32, (L, L), 1)
    causal = col <= row
    ones = jnp.ones((L, LANES), BF16)

    for r in range(nb):
        qc = conv_silu(q_ref[r].astype(F32), qtails[r], qtail, r, cw[:, :W], cb[:, :W])
        kc = conv_silu(k_ref[r].astype(F32), ktails[r], ktail, r, cw[:, W:], cb[:, W:])
        kc = kc * (1.0 / math.sqrt(ML_HEAD_DIM))
        ifc = ifc_ref[r] + bic_ref[...]
        ifr = ifr_refs[r][...] + bir_ref[...]

        for h in range(ML_HEADS):
            st = r * ML_HEADS + h
            sl = slice(h * ML_HEAD_DIM, (h + 1) * ML_HEAD_DIM)
            i_col = ifc[:, h:h + 1]
            f_col = ifc[:, ML_HEADS + h:ML_HEADS + h + 1]
            i_row = ifr[h:h + 1, :]
            f_row = ifr[ML_HEADS + h:ML_HEADS + h + 1, :]
            lf_col = jnp.minimum(f_col, 0.0) - jnp.log(1.0 + jnp.exp(-jnp.abs(f_col)))
            lf_row = jnp.minimum(f_row, 0.0) - jnp.log(1.0 + jnp.exp(-jnp.abs(f_row)))
            b_col = jnp.sum(jnp.where(causal, lf_row, 0.0), axis=1, keepdims=True)
            b_row = jnp.sum(jnp.where(row <= col, lf_col, 0.0), axis=0, keepdims=True)
            b_last = jnp.sum(lf_row, axis=1, keepdims=True)
            m_st = m_all[st, 0:1, 0:1]

            log_d = jnp.where(causal, b_col - b_row + i_row, -jnp.inf)
            m_inter = b_col + m_st
            m_t = jnp.maximum(m_inter, jnp.max(log_d, axis=1, keepdims=True))
            q_f = qc[:, sl]
            k_h = kc[:, sl]
            v_ext = jnp.concatenate([v_ref[r, :, sl], ones], axis=1)
            s = _dot_nt(q_f.astype(BF16), k_h.astype(BF16)) * jnp.exp(log_d - m_t)
            w_inter = jnp.exp(m_inter - m_t)
            ct = ct_all[st]
            num = _dot(jnp.concatenate([s.astype(BF16), (w_inter * q_f).astype(BF16)], axis=1),
                       jnp.concatenate([v_ext, ct.astype(BF16)], axis=0))
            den = num[:, ML_HEAD_DIM:]
            hh = num[:, :ML_HEAD_DIM] / jnp.maximum(jnp.abs(den), jnp.exp(-m_t))

            log_g = b_last - b_col + i_col
            m_new = jnp.maximum(b_last + m_st, jnp.max(log_g, axis=0, keepdims=True))
            decay = jnp.exp(b_last + m_st - m_new)
            wk = jnp.exp(log_g - m_new)
            stores.append((ct_scr, st, decay * ct + _dot_tn((wk * k_h).astype(BF16), v_ext)))
            stores.append((m_scr, st, jnp.broadcast_to(m_new, (SUBLANES, LANES))))

            hn = hh * lax.rsqrt(jnp.mean(hh * hh, axis=-1, keepdims=True) + EPS) * gout_ref[:, sl]
            stores.append((y_ref, (r, slice(None), sl), (_sigmoid(o_ref[r, :, sl].astype(F32)) * hn).astype(BF16)))

    for ref, idx, val in stores:
        ref[idx] = val


def _mlstm(proj, ifc, ifr, conv_w, conv_b, bias_col, bias_row, g_out, bsz, seq, cols, nb):
    qcol, kcol, vcol, ocol = cols
    L = ML_CHUNK
    W = ML_HEADS * ML_HEAD_DIM
    nc = seq // L
    proj3 = proj.reshape(bsz, seq, proj.shape[1])
    ifc3 = ifc.reshape(bsz, seq, LANES)

    def colspec(cb):
        return pl.BlockSpec((nb, L, W), lambda b, c: (b, c, cb))

    def const(shape):
        return pl.BlockSpec(shape, lambda b, c: (0,) * len(shape))

    ifr_specs = [pl.BlockSpec((SUBLANES, L), lambda b, c, r=r: (0, (b * nb + r) * nc + c)) for r in range(nb)]
    y = pl.pallas_call(
        functools.partial(_mlstm_kernel, nb=nb),
        grid=(bsz // nb, nc),
        in_specs=[
            colspec(qcol), colspec(kcol), colspec(vcol), colspec(ocol),
            pl.BlockSpec((nb, L, LANES), lambda b, c: (b, c, 0)),
            *ifr_specs,
            const((ML_CONV, 2 * W)), const((1, 2 * W)),
            const((1, LANES)), const((SUBLANES, 1)), const((1, W)),
        ],
        out_specs=pl.BlockSpec((nb, L, W), lambda b, c: (b, c, 0)),
        out_shape=jax.ShapeDtypeStruct((bsz, seq, W), BF16),
        scratch_shapes=[
            pltpu.VMEM((nb, SUBLANES, W), F32),
            pltpu.VMEM((nb, SUBLANES, W), F32),
            pltpu.VMEM((nb * ML_HEADS, ML_HEAD_DIM, ML_HEAD_DIM + LANES), F32),
            pltpu.VMEM((nb * ML_HEADS, SUBLANES, LANES), F32),
        ],
        compiler_params=_cparams(("arbitrary", "arbitrary")),
        name="mlstm",
    )(proj3, proj3, proj3, proj3, ifc3, *([ifr] * nb), conv_w, conv_b, bias_col, bias_row, g_out)
    return y.reshape(bsz * seq, W)


def _memkv_kernel(mem_ref, g_ref, w_ref, gk_ref, k_ref, v_ref):
    x = mem_ref[...]
    ms = jnp.mean(x * x, axis=-1, keepdims=True)
    mn = (x * lax.rsqrt(ms + EPS) * g_ref[...]).astype(BF16)
    kv = _dot(mn, w_ref[...])
    W = XA_HEADS * XA_HEAD_DIM
    for h in range(XA_HEADS):
        sl = slice(h * XA_HEAD_DIM, (h + 1) * XA_HEAD_DIM)
        kh = kv[:, sl]
        kn = kh * lax.rsqrt(jnp.mean(kh * kh, axis=-1, keepdims=True) + EPS) * gk_ref[...]
        k_ref[:, sl] = kn.astype(BF16)
    v_ref[...] = kv[:, W:].astype(BF16)


def _memkv(mem2, g_mem, w_kv, g_k, bsz, n_mem):
    d = mem2.shape[1]
    W = XA_HEADS * XA_HEAD_DIM
    return pl.pallas_call(
        _memkv_kernel,
        grid=(bsz,),
        in_specs=[
            pl.BlockSpec((n_mem, d), lambda b: (b, 0)),
            pl.BlockSpec((1, d), lambda b: (0, 0)),
            pl.BlockSpec((d, 2 * W), lambda b: (0, 0)),
            pl.BlockSpec((1, XA_HEAD_DIM), lambda b: (0, 0)),
        ],
        out_specs=[pl.BlockSpec((n_mem, W), lambda b: (b, 0)),
                   pl.BlockSpec((n_mem, W), lambda b: (b, 0))],
        out_shape=[jax.ShapeDtypeStruct((bsz * n_mem, W), BF16)] * 2,
        compiler_params=_cparams(("arbitrary",)),
        name="memkv",
    )(mem2, g_mem, w_kv, g_k)


def _merge_kernel(x_ref, ysb_ref, yml_ref, xq_ref, kn_ref, vv_ref, gq_ref, gm_ref, wg_ref,
                  wsb_ref, wml_ref, wxa_ref, wo_ref, gf_ref, x1_ref, h2_ref):
    scale = 1.0 / math.sqrt(XA_HEAD_DIM)
    heads = []
    for hd in range(XA_HEADS):
        sl = slice(hd * XA_HEAD_DIM, (hd + 1) * XA_HEAD_DIM)
        qh = xq_ref[:, sl].astype(F32)
        qn = qh * lax.rsqrt(jnp.mean(qh * qh, axis=-1, keepdims=True) + EPS) * gq_ref[...]
        s = _dot_nt(qn.astype(BF16), kn_ref[:, sl]) * scale
        p = jnp.exp(s - jnp.max(s, axis=-1, keepdims=True))
        p = p / jnp.sum(p, axis=-1, keepdims=True)
        heads.append(_dot(p.astype(BF16), vv_ref[:, sl]).astype(BF16))
    yxa = jnp.concatenate(heads, axis=1)

    x = x_ref[...]
    d = x.shape[1]
    h = (x * lax.rsqrt(jnp.mean(x * x, axis=-1, keepdims=True) + EPS) * gm_ref[...]).astype(BF16)
    merged = None
    for br, (y, w_ref) in enumerate(((ysb_ref[...], wsb_ref), (yml_ref[...], wml_ref), (yxa, wxa_ref))):
        gate = _sigmoid(_dot(h, wg_ref[:, br * d:(br + 1) * d]))
        term = gate * _dot(y, w_ref[...])
        merged = term if merged is None else merged + term
    x1 = x + _dot(merged.astype(BF16), wo_ref[...])
    x1_ref[...] = x1
    ms = jnp.mean(x1 * x1, axis=-1, keepdims=True)
    h2_ref[...] = (x1 * lax.rsqrt(ms + EPS) * gf_ref[...]).astype(BF16)


def _merge(x2, ysb, yml, proj, xq_col, kn, vv, g_q, g_mix, w_gate, w_sb, w_ml, w_xa, w_o, g_ffn, seq, tm):
    t, d = x2.shape
    wb = ysb.shape[1]
    n_mem = kn.shape[0] // (t // seq)
    tiles_per_seq = seq // tm

    def rows(width, cb=0):
        return pl.BlockSpec((tm, width), lambda i: (i, cb))

    def const(shape):
        return pl.BlockSpec(shape, lambda i: (0, 0), pipeline_mode=pl.Buffered(1))

    def per_batch(shape):
        return pl.BlockSpec(shape, lambda i: (i // tiles_per_seq, 0))

    return pl.pallas_call(
        _merge_kernel,
        grid=(t // tm,),
        in_specs=[
            rows(d), rows(wb), rows(wb), rows(wb, xq_col),
            per_batch((n_mem, wb)), per_batch((n_mem, wb)), const((1, XA_HEAD_DIM)),
            const((1, d)), const((d, 3 * d)),
            const((wb, d)), const((wb, d)), const((wb, d)), const((d, d)), const((1, d)),
        ],
        out_specs=[rows(d), rows(d)],
        out_shape=[jax.ShapeDtypeStruct((t, d), F32), jax.ShapeDtypeStruct((t, d), BF16)],
        compiler_params=_cparams(("arbitrary",)),
        name="merge",
    )(x2, ysb, yml, proj, kn, vv, g_q, g_mix, w_gate, w_sb, w_ml, w_xa, w_o, g_ffn)


def _ffn_kernel(x1_ref, h2_ref, wup_ref, cw_ref, cb_ref, wd_ref, o_ref, tail_scr, act_scr,
                *, tm, tf, tiles_per_seq):
    i = pl.program_id(0)
    first = (i % tiles_per_seq) == 0
    th = tm // FFN_ROW_SPLIT

    for hh in range(FFN_ROW_SPLIT):
        rows = slice(hh * th, (hh + 1) * th)
        h2 = h2_ref[rows, :]
        for co in range(0, tf, LANES):
            cols = slice(2 * co, 2 * co + 2 * LANES)
            u = _dot(h2, wup_ref[:, cols])
            tail = tail_scr[:, cols]
            if hh == 0:
                tail = jnp.where(first, jnp.zeros_like(tail), tail)
            tail_scr[:, cols] = u[th - SUBLANES:th, :]
            head = jnp.concatenate([tail, u[0:SUBLANES, :]], axis=0)
            cw = cw_ref[:, cols]
            y = cb_ref[:, cols] + cw[FF_CONV - 1:FF_CONV, :] * u
            for back in range(1, FF_CONV):
                prev = jnp.concatenate(
                    [pltpu.roll(head, back, axis=0)[SUBLANES:, :], pltpu.roll(u, back, axis=0)[SUBLANES:, :]],
                    axis=0)
                y = y + cw[FF_CONV - 1 - back:FF_CONV - back, :] * prev
            uv, ug = y[:, :LANES], y[:, LANES:]
            act_scr[rows, co:co + LANES] = (ug * _sigmoid(ug) * uv).astype(BF16)
        o_ref[rows, :] = x1_ref[rows, :] + _dot(act_scr[rows, :], wd_ref[...])


def _interleave_val_gate(a):
    dff = a.shape[-1] // 2
    blocks = []
    for c in range(dff // LANES):
        blocks.append(a[..., c * LANES:(c + 1) * LANES])
        blocks.append(a[..., dff + c * LANES:dff + (c + 1) * LANES])
    return jnp.concatenate(blocks, axis=-1)


def _ffn(x1, h2, w_up, conv_w, conv_b, w_down, seq, tm):
    t, d = x1.shape
    dff = w_down.shape[0]

    def resident(shape):
        return pl.BlockSpec(shape, lambda i: (0, 0), pipeline_mode=pl.Buffered(1))

    return pl.pallas_call(
        functools.partial(_ffn_kernel, tm=tm, tf=dff, tiles_per_seq=seq // tm),
        grid=(t // tm,),
        in_specs=[
            pl.BlockSpec((tm, d), lambda i: (i, 0)),
            pl.BlockSpec((tm, d), lambda i: (i, 0)),
            resident((d, 2 * dff)),
            resident((FF_CONV, 2 * dff)),
            resident((1, 2 * dff)),
            resident((dff, d)),
        ],
        out_specs=pl.BlockSpec((tm, d), lambda i: (i, 0)),
        out_shape=jax.ShapeDtypeStruct((t, d), F32),
        scratch_shapes=[
            pltpu.VMEM((SUBLANES, 2 * dff), F32),
            pltpu.VMEM((tm, dff), BF16),
        ],
        compiler_params=_cparams(("arbitrary",)),
        name="ffn",
    )(x1, h2, _interleave_val_gate(w_up), _interleave_val_gate(conv_w), _interleave_val_gate(conv_b), w_down)


def kernel(x, mem, g_mix, w_in, ml_conv_w, ml_conv_b, ml_b_i, ml_b_f, ml_g_out, g_mem,
           w_mem_kv, xa_g_q, xa_g_k, w_sb_out, w_ml_out, w_xa_out, w_o, g_ffn, w_up,
           ff_conv_w, ff_conv_b, w_down):
    bsz, seq, d = x.shape
    n_mem = mem.shape[1]
    depth = w_in.shape[0]
    sbw = SB_HEADS * SB_HEAD_DIM
    mlw = ML_HEADS * ML_HEAD_DIM
    xaw = XA_HEADS * XA_HEAD_DIM
    n_wide = 3 * sbw + 4 * mlw
    if_lo, if_hi = n_wide, n_wide + 2 * ML_HEADS
    sb_scale = math.log2(math.e) / math.sqrt(SB_HEAD_DIM)

    x2 = x.reshape(bsz * seq, d)
    mem2 = mem.reshape(bsz * n_mem, d)
    for l in range(depth):
        wl = w_in[l]
        w_main = jnp.concatenate(
            [wl[:, :sbw] * (-sb_scale), wl[:, sbw:n_wide], wl[:, if_hi:if_hi + xaw]], axis=1).astype(BF16)
        w_gate = wl[:, if_hi + xaw:].astype(BF16)
        w_if = wl[:, if_lo:if_hi]
        w_ifc = jnp.pad(w_if, ((0, 0), (0, LANES - 2 * ML_HEADS))).astype(BF16)
        proj, ifc, ifr = _inproj(x2, g_mix[l][None, :], w_main, w_ifc, tm=512, tn=2048)

        ysb = _sb_attention(proj, bsz, seq, qcol=0, kcol=sbw // LANES, vcol=2 * sbw // LANES, tq=512)

        bias = jnp.concatenate([ml_b_i[l], ml_b_f[l]])
        bias_col = jnp.pad(bias, (0, LANES - 2 * ML_HEADS))[None, :]
        bias_row = bias[:, None]
        ml0 = 3 * sbw // mlw
        yml = _mlstm(proj, ifc, ifr, ml_conv_w[l], ml_conv_b[l][None, :], bias_col, bias_row,
                     ml_g_out[l].reshape(1, mlw), bsz, seq, cols=(ml0, ml0 + 1, ml0 + 2, ml0 + 3), nb=ML_BATCH)

        kn, vv = _memkv(mem2, g_mem[l][None, :], w_mem_kv[l].astype(BF16), xa_g_k[l][None, :], bsz, n_mem)
        x1, h2 = _merge(x2, ysb, yml, proj, n_wide // xaw, kn, vv, xa_g_q[l][None, :],
                        g_mix[l][None, :], w_gate,
                        w_sb_out[l].astype(BF16), w_ml_out[l].astype(BF16), w_xa_out[l].astype(BF16),
                        w_o[l].astype(BF16), g_ffn[l][None, :], seq, tm=512)

        x2 = _ffn(x1, h2, w_up[l].astype(BF16), ff_conv_w[l], ff_conv_b[l][None, :],
                  w_down[l].astype(BF16), seq, tm=512)
    return x2.reshape(bsz, seq, d)
```

```python
import functools
import math

import jax
import jax.numpy as jnp
from jax import lax
from jax.experimental import pallas as pl
from jax.experimental.pallas import tpu as pltpu

F32 = jnp.float32
BF16 = jnp.bfloat16

SB_HEADS = 8
SB_HEAD_DIM = 64
ML_HEADS = 4
ML_HEAD_DIM = 128
ML_CONV = 4
ML_CHUNK = 128
XA_HEADS = 4
XA_HEAD_DIM = 128
FF_CONV = 3
EPS = 1e-6
SB_UNROLL = 2
SB_DEAD_LOG2 = -150.0
ML_BATCH = 1
FFN_ROW_SPLIT = 2

LANES = 128
SUBLANES = 8
VMEM_LIMIT = 56 * 1024 * 1024


def _cparams(sem):
    return pltpu.CompilerParams(dimension_semantics=sem, vmem_limit_bytes=VMEM_LIMIT)


def _dot(a, b):
    return jnp.dot(a, b, preferred_element_type=F32)


def _dot_nt(a, b):
    return lax.dot_general(a, b, (((1,), (1,)), ((), ())), preferred_element_type=F32)


def _dot_tn(a, b):
    return lax.dot_general(a, b, (((0,), (0,)), ((), ())), preferred_element_type=F32)


def _sigmoid(x):
    return 0.5 * jnp.tanh(0.5 * x) + 0.5


def _inproj_kernel(x_ref, g_ref, w_ref, wif_ref, o_ref, ifc_ref, ifr_ref, h_scr, *, tn):
    j = pl.program_id(1)

    @pl.when(j == 0)
    def _():
        x = x_ref[...]
        ms = jnp.mean(x * x, axis=-1, keepdims=True)
        h = (x * lax.rsqrt(ms + EPS) * g_ref[...]).astype(BF16)
        h_scr[...] = h
        ifc = _dot_nt(h, wif_ref[...])
        ifc_ref[...] = ifc
        ifr_ref[...] = ifc.T[0:SUBLANES, :]

    o_ref[...] = _dot_nt(h_scr[...], w_ref[pl.ds(pl.multiple_of(j * tn, LANES), tn), :]).astype(BF16)


def _inproj(x2, g, w_main, w_if, tm, tn):
    t, d = x2.shape
    n = w_main.shape[0]
    return pl.pallas_call(
        functools.partial(_inproj_kernel, tn=tn),
        grid=(t // tm, n // tn),
        in_specs=[
            pl.BlockSpec((tm, d), lambda i, j: (i, 0)),
            pl.BlockSpec((1, d), lambda i, j: (0, 0)),
            pl.BlockSpec((n, d), lambda i, j: (0, 0), pipeline_mode=pl.Buffered(1)),
            pl.BlockSpec((LANES, d), lambda i, j: (0, 0)),
        ],
        out_specs=[
            pl.BlockSpec((tm, tn), lambda i, j: (i, j)),
            pl.BlockSpec((tm, LANES), lambda i, j: (i, 0)),
            pl.BlockSpec((SUBLANES, tm), lambda i, j: (0, i)),
        ],
        out_shape=[
            jax.ShapeDtypeStruct((t, n), BF16),
            jax.ShapeDtypeStruct((t, LANES), F32),
            jax.ShapeDtypeStruct((SUBLANES, t), F32),
        ],
        scratch_shapes=[pltpu.VMEM((tm, d), BF16)],
        compiler_params=_cparams(("arbitrary", "arbitrary")),
        name="inproj",
    )(x2, g, w_main, w_if)


def _sb_kernel(q_ref, k_ref, v_ref, o_ref, kk_scr, vv_scr, acc_scr, carry_scr, *, tq):
    i = pl.program_id(2)
    tk = LANES
    nsub = tq // tk
    seq = k_ref.shape[0]

    @pl.when(i == 0)
    def _():
        lane = lax.broadcasted_iota(jnp.int32, (seq, LANES), 1)
        for src, dst in ((k_ref, kk_scr), (v_ref, vv_scr)):
            a = src[...]
            zero = jnp.zeros_like(a)
            dst[:, 0:tk, :] = jnp.where(lane < SB_HEAD_DIM, a, zero).reshape(seq // tk, tk, LANES)
            dst[:, tk:2 * tk, :] = jnp.where(lane >= SB_HEAD_DIM, a, zero).reshape(seq // tk, tk, LANES)

    r2 = lax.broadcasted_iota(jnp.int32, (2 * tk, 2 * tk), 0)
    c2 = lax.broadcasted_iota(jnp.int32, (2 * tk, 2 * tk), 1)
    suffix = jnp.where(((r2 // tk) == (c2 // tk)) & ((r2 % tk) > (c2 % tk)), 1.0, 0.0).astype(BF16)

    acc_scr[...] = jnp.zeros_like(acc_scr)
    carry_scr[...] = jnp.zeros_like(carry_scr)

    def step(kb_lo, nblk, tile_off):
        q = q_ref[...]
        kk = kk_scr[pl.ds(kb_lo, nblk)].reshape(nblk * 2 * tk, LANES)
        y = _dot_nt(q, kk)
        neg_abs = lax.bitcast_convert_type(
            lax.bitcast_convert_type(y, jnp.uint32) | jnp.uint32(0x80000000), F32)
        log_fail = jnp.minimum(y, 0.0) - jnp.log2(1.0 + jnp.exp2(neg_abs))
        rr = lax.broadcasted_iota(jnp.int32, (tq, 2 * tk), 0)
        cc = lax.broadcasted_iota(jnp.int32, (tq, 2 * tk), 1) % tk
        carry = carry_scr[...]
        ws = [None] * nblk
        for b in reversed(range(nblk)):
            sl = slice(b * 2 * tk, (b + 1) * 2 * tk)
            lf = log_fail[:, sl]
            strict = None
            if tile_off is not None and tile_off + b >= 0:
                strict = cc + (tile_off + b) * tk < rr
                lf = jnp.where(strict, lf, 0.0)
            suf = _dot(lf.astype(BF16), suffix)
            w = jnp.exp2((lf - y[:, sl]) + (suf + carry))
            if strict is not None:
                w = jnp.where(strict, w, 0.0)
            ws[b] = w.astype(BF16)
            tot0 = jnp.sum(lf[:, :tk], axis=1, keepdims=True)
            tot1 = jnp.sum(lf[:, tk:], axis=1, keepdims=True)
            carry = carry + jnp.concatenate(
                [jnp.broadcast_to(tot0, (tq, tk)), jnp.broadcast_to(tot1, (tq, tk))], axis=1)
        vv = vv_scr[pl.ds(kb_lo, nblk)].reshape(nblk * 2 * tk, LANES)
        acc_scr[...] += _dot(jnp.concatenate(ws, axis=1), vv)
        carry_scr[...] = carry

    @pl.when(i == 0)
    def _():
        step(0, nsub, 0)

    @pl.when(i > 0)
    def _():
        step(i * nsub - SB_UNROLL, nsub + SB_UNROLL, -SB_UNROLL)

    n_steps = jnp.maximum(i * nsub // SB_UNROLL - 1, 0)

    def cond(state):
        it, live = state
        return jnp.logical_and(it < n_steps, live)

    def body(state):
        it, _ = state
        step(i * nsub - SB_UNROLL * (it + 2), SB_UNROLL, None)
        return it + 1, jnp.max(carry_scr[...]) >= SB_DEAD_LOG2

    lax.while_loop(cond, body, (jnp.int32(0), jnp.max(carry_scr[...]) >= SB_DEAD_LOG2))
    o_ref[...] = acc_scr[...].astype(BF16)


def _sb_attention(proj, bsz, seq, qcol, kcol, vcol, tq):
    npair = SB_HEADS * SB_HEAD_DIM // LANES
    nq = seq // tq
    return pl.pallas_call(
        functools.partial(_sb_kernel, tq=tq),
        grid=(bsz, npair, nq),
        in_specs=[
            pl.BlockSpec((tq, LANES), lambda b, p, i: (b * nq + i, qcol + p)),
            pl.BlockSpec((seq, LANES), lambda b, p, i: (b, kcol + p)),
            pl.BlockSpec((seq, LANES), lambda b, p, i: (b, vcol + p)),
        ],
        out_specs=pl.BlockSpec((tq, LANES), lambda b, p, i: (b * nq + i, p)),
        out_shape=jax.ShapeDtypeStruct((bsz * seq, npair * LANES), BF16),
        scratch_shapes=[
            pltpu.VMEM((seq // LANES, 2 * LANES, LANES), BF16),
            pltpu.VMEM((seq // LANES, 2 * LANES, LANES), BF16),
            pltpu.VMEM((tq, LANES), F32),
            pltpu.VMEM((tq, 2 * LANES), F32),
        ],
        compiler_params=_cparams(("arbitrary", "arbitrary", "arbitrary")),
        name="sb_attn",
    )(proj, proj, proj)


def _mlstm_kernel(*refs, nb):
    q_ref, k_ref, v_ref, o_ref, ifc_ref = refs[:5]
    ifr_refs = refs[5:5 + nb]
    cw_ref, cb_ref, bic_ref, bir_ref, gout_ref, y_ref, qtail, ktail, ct_scr, m_scr = refs[5 + nb:]
    c = pl.program_id(1)
    L = ML_CHUNK
    W = ML_HEADS * ML_HEAD_DIM

    @pl.when(c == 0)
    def _():
        qtail[...] = jnp.zeros_like(qtail)
        ktail[...] = jnp.zeros_like(ktail)
        ct_scr[...] = jnp.zeros_like(ct_scr)
        m_scr[...] = jnp.zeros_like(m_scr)

    stores = []
    m_all = m_scr[...]
    ct_all = [ct_scr[st] for st in range(nb * ML_HEADS)]
    qtails = [qtail[r] for r in range(nb)]
    ktails = [ktail[r] for r in range(nb)]

    def conv_silu(x, tail, tail_ref, r, w, b):
        head = jnp.concatenate([tail, x[0:SUBLANES, :]], axis=0)
        stores.append((tail_ref, r, x[L - SUBLANES:L, :]))
        y = b + w[ML_CONV - 1:ML_CONV, :] * x
        for back in range(1, ML_CONV):
            prev = jnp.concatenate(
                [pltpu.roll(head, back, axis=0)[SUBLANES:, :], pltpu.roll(x, back, axis=0)[SUBLANES:, :]],
                axis=0)
            y = y + w[ML_CONV - 1 - back:ML_CONV - back, :] * prev
        return y * _sigmoid(y)

    cw = cw_ref[...]
    cb = cb_ref[...]
    row = lax.broadcasted_iota(jnp.int32, (L, L), 0)
    col = lax.broadcasted_iota(jnp.int32, (L, L), 1)
    causal = col <= row
    ones = jnp.ones((L, LANES), BF16)

    for r in range(nb):
        qc = conv_silu(q_ref[r].astype(F32), qtails[r], qtail, r, cw[:, :W], cb[:, :W])
        kc = conv_silu(k_ref[r].astype(F32), ktails[r], ktail, r, cw[:, W:], cb[:, W:])
        kc = kc * (1.0 / math.sqrt(ML_HEAD_DIM))
        ifc = ifc_ref[r] + bic_ref[...]
        ifr = ifr_refs[r][...] + bir_ref[...]

        for h in range(ML_HEADS):
            st = r * ML_HEADS + h
            sl = slice(h * ML_HEAD_DIM, (h + 1) * ML_HEAD_DIM)
            i_col = ifc[:, h:h + 1]
            f_col = ifc[:, ML_HEADS + h:ML_HEADS + h + 1]
            i_row = ifr[h:h + 1, :]
            f_row = ifr[ML_HEADS + h:ML_HEADS + h + 1, :]
            lf_col = jnp.minimum(f_col, 0.0) - jnp.log(1.0 + jnp.exp(-jnp.abs(f_col)))
            lf_row = jnp.minimum(f_row, 0.0) - jnp.log(1.0 + jnp.exp(-jnp.abs(f_row)))
            b_col = jnp.sum(jnp.where(causal, lf_row, 0.0), axis=1, keepdims=True)
            b_row = jnp.sum(jnp.where(row <= col, lf_col, 0.0), axis=0, keepdims=True)
            b_last = jnp.sum(lf_row, axis=1, keepdims=True)
            m_st = m_all[st, 0:1, 0:1]
            log_d = jnp.where(causal, b_col - b_row + i_row, -jnp.inf)
            m_inter = b_col + m_st
            m_t = jnp.maximum(m_inter, jnp.max(log_d, axis=1, keepdims=True))
            q_f = qc[:, sl]
            k_h = kc[:, sl]
            v_ext = jnp.concatenate([v_ref[r, :, sl], ones], axis=1)
            s = _dot_nt(q_f.astype(BF16), k_h.astype(BF16)) * jnp.exp(log_d - m_t)
            w_inter = jnp.exp(m_inter - m_t)
            ct = ct_all[st]
            num = _dot(jnp.concatenate([s.astype(BF16), (w_inter * q_f).astype(BF16)], axis=1),
                       jnp.concatenate([v_ext, ct.astype(BF16)], axis=0))
            den = num[:, ML_HEAD_DIM:]
            hh = num[:, :ML_HEAD_DIM] / jnp.maximum(jnp.abs(den), jnp.exp(-m_t))

            log_g = b_last - b_col + i_col
            m_new = jnp.maximum(b_last + m_st, jnp.max(log_g, axis=0, keepdims=True))
            decay = jnp.exp(b_last + m_st - m_new)
            wk = jnp.exp(log_g - m_new)
            stores.append((ct_scr, st, decay * ct + _dot_tn((wk * k_h).astype(BF16), v_ext)))
            stores.append((m_scr, st, jnp.broadcast_to(m_new, (SUBLANES, LANES))))

            hn = hh * lax.rsqrt(jnp.mean(hh * hh, axis=-1, keepdims=True) + EPS) * gout_ref[:, sl]
            stores.append((y_ref, (r, slice(None), sl), (_sigmoid(o_ref[r, :, sl].astype(F32)) * hn).astype(BF16)))

    for ref, idx, val in stores:
        ref[idx] = val


def _mlstm(proj, ifc, ifr, conv_w, conv_b, bias_col, bias_row, g_out, bsz, seq, cols, nb):
    qcol, kcol, vcol, ocol = cols
    L = ML_CHUNK
    W = ML_HEADS * ML_HEAD_DIM
    nc = seq // L
    proj3 = proj.reshape(bsz, seq, proj.shape[1])
    ifc3 = ifc.reshape(bsz, seq, LANES)

    def colspec(cb):
        return pl.BlockSpec((nb, L, W), lambda b, c: (b, c, cb))

    def const(shape):
        return pl.BlockSpec(shape, lambda b, c: (0,) * len(shape))

    ifr_specs = [pl.BlockSpec((SUBLANES, L), lambda b, c, r=r: (0, (b * nb + r) * nc + c)) for r in range(nb)]
    y = pl.pallas_call(
        functools.partial(_mlstm_kernel, nb=nb),
        grid=(bsz // nb, nc),
        in_specs=[
            colspec(qcol), colspec(kcol), colspec(vcol), colspec(ocol),
            pl.BlockSpec((nb, L, LANES), lambda b, c: (b, c, 0)),
            *ifr_specs,
            const((ML_CONV, 2 * W)), const((1, 2 * W)),
            const((1, LANES)), const((SUBLANES, 1)), const((1, W)),
        ],
        out_specs=pl.BlockSpec((nb, L, W), lambda b, c: (b, c, 0)),
        out_shape=jax.ShapeDtypeStruct((bsz, seq, W), BF16),
        scratch_shapes=[
            pltpu.VMEM((nb, SUBLANES, W), F32),
            pltpu.VMEM((nb, SUBLANES, W), F32),
            pltpu.VMEM((nb * ML_HEADS, ML_HEAD_DIM, ML_HEAD_DIM + LANES), F32),
            pltpu.VMEM((nb * ML_HEADS, SUBLANES, LANES), F32),
        ],
        compiler_params=_cparams(("arbitrary", "arbitrary")),
        name="mlstm",
    )(proj3, proj3, proj3, proj3, ifc3, *([ifr] * nb), conv_w, conv_b, bias_col, bias_row, g_out)
    return y.reshape(bsz * seq, W)


def _memkv_kernel(mem_ref, g_ref, w_ref, gk_ref, k_ref, v_ref):
    x = mem_ref[...]
    ms = jnp.mean(x * x, axis=-1, keepdims=True)
    mn = (x * lax.rsqrt(ms + EPS) * g_ref[...]).astype(BF16)
    kv = _dot(mn, w_ref[...])
    W = XA_HEADS * XA_HEAD_DIM
    for h in range(XA_HEADS):
        sl = slice(h * XA_HEAD_DIM, (h + 1) * XA_HEAD_DIM)
        kh = kv[:, sl]
        kn = kh * lax.rsqrt(jnp.mean(kh * kh, axis=-1, keepdims=True) + EPS) * gk_ref[...]
        k_ref[:, sl] = kn.astype(BF16)
    v_ref[...] = kv[:, W:].astype(BF16)


def _memkv(mem2, g_mem, w_kv, g_k, bsz, n_mem):
    d = mem2.shape[1]
    W = XA_HEADS * XA_HEAD_DIM
    return pl.pallas_call(
        _memkv_kernel,
        grid=(bsz,),
        in_specs=[
            pl.BlockSpec((n_mem, d), lambda b: (b, 0)),
            pl.BlockSpec((1, d), lambda b: (0, 0)),
            pl.BlockSpec((d, 2 * W), lambda b: (0, 0)),
            pl.BlockSpec((1, XA_HEAD_DIM), lambda b: (0, 0)),
        ],
        out_specs=[pl.BlockSpec((n_mem, W), lambda b: (b, 0)),
                   pl.BlockSpec((n_mem, W), lambda b: (b, 0))],
        out_shape=[jax.ShapeDtypeStruct((bsz * n_mem, W), BF16)] * 2,
        compiler_params=_cparams(("arbitrary",)),
        name="memkv",
    )(mem2, g_mem, w_kv, g_k)


def _merge_kernel(x_ref, ysb_ref, yml_ref, xq_ref, kn_ref, vv_ref, gq_ref, gm_ref, wg_ref,
                  wsb_ref, wml_ref, wxa_ref, wo_ref, gf_ref, x1_ref, h2_ref):
    scale = 1.0 / math.sqrt(XA_HEAD_DIM)
    heads = []
    for hd in range(XA_HEADS):
        sl = slice(hd * XA_HEAD_DIM, (hd + 1) * XA_HEAD_DIM)
        qh = xq_ref[:, sl].astype(F32)
        qn = qh * lax.rsqrt(jnp.mean(qh * qh, axis=-1, keepdims=True) + EPS) * gq_ref[...]
        s = _dot_nt(qn.astype(BF16), kn_ref[:, sl]) * scale
        p = jnp.exp(s - jnp.max(s, axis=-1, keepdims=True))
        p = p / jnp.sum(p, axis=-1, keepdims=True)
        heads.append(_dot(p.astype(BF16), vv_ref[:, sl]).astype(BF16))
    yxa = jnp.concatenate(heads, axis=1)

    x = x_ref[...]
    d = x.shape[1]
    h = (x * lax.rsqrt(jnp.mean(x * x, axis=-1, keepdims=True) + EPS) * gm_ref[...]).astype(BF16)
    merged = None
    for br, (y, w_ref) in enumerate(((ysb_ref[...], wsb_ref), (yml_ref[...], wml_ref), (yxa, wxa_ref))):
        gate = _sigmoid(_dot_nt(h, wg_ref[br * d:(br + 1) * d, :]))
        term = gate * _dot(y, w_ref[...])
        merged = term if merged is None else merged + term
    x1 = x + _dot(merged.astype(BF16), wo_ref[...])
    x1_ref[...] = x1
    ms = jnp.mean(x1 * x1, axis=-1, keepdims=True)
    h2_ref[...] = (x1 * lax.rsqrt(ms + EPS) * gf_ref[...]).astype(BF16)


def _merge(x2, ysb, yml, proj, xq_col, kn, vv, g_q, g_mix, w_gate, w_sb, w_ml, w_xa, w_o, g_ffn, seq, tm):
    t, d = x2.shape
    wb = ysb.shape[1]
    n_mem = kn.shape[0] // (t // seq)
    tiles_per_seq = seq // tm

    def rows(width, cb=0):
        return pl.BlockSpec((tm, width), lambda i: (i, cb))

    def const(shape):
        return pl.BlockSpec(shape, lambda i: (0, 0), pipeline_mode=pl.Buffered(1))

    def per_batch(shape):
        return pl.BlockSpec(shape, lambda i: (i // tiles_per_seq, 0))

    return pl.pallas_call(
        _merge_kernel,
        grid=(t // tm,),
        in_specs=[
            rows(d), rows(wb), rows(wb), rows(wb, xq_col),
            per_batch((n_mem, wb)), per_batch((n_mem, wb)), const((1, XA_HEAD_DIM)),
            const((1, d)), const((3 * d, d)),
            const((wb, d)), const((wb, d)), const((wb, d)), const((d, d)), const((1, d)),
        ],
        out_specs=[rows(d), rows(d)],
        out_shape=[jax.ShapeDtypeStruct((t, d), F32), jax.ShapeDtypeStruct((t, d), BF16)],
        compiler_params=_cparams(("arbitrary",)),
        name="merge",
    )(x2, ysb, yml, proj, kn, vv, g_q, g_mix, w_gate, w_sb, w_ml, w_xa, w_o, g_ffn)


def _ffn_kernel(x1_ref, h2_ref, wup_ref, cw_ref, cb_ref, wd_ref, o_ref, tail_scr, act_scr,
                *, tm, tf, tiles_per_seq):
    i = pl.program_id(0)
    first = (i % tiles_per_seq) == 0
    th = tm // FFN_ROW_SPLIT

    for hh in range(FFN_ROW_SPLIT):
        rows = slice(hh * th, (hh + 1) * th)
        h2 = h2_ref[rows, :]
        for co in range(0, tf, LANES):
            cols = slice(2 * co, 2 * co + 2 * LANES)
            u = _dot(h2, wup_ref[:, cols])
            tail = tail_scr[:, cols]
            if hh == 0:
                tail = jnp.where(first, jnp.zeros_like(tail), tail)
            tail_scr[:, cols] = u[th - SUBLANES:th, :]
            head = jnp.concatenate([tail, u[0:SUBLANES, :]], axis=0)
            cw = cw_ref[:, cols]
            y = cb_ref[:, cols] + cw[FF_CONV - 1:FF_CONV, :] * u
            for back in range(1, FF_CONV):
                prev = jnp.concatenate(
                    [pltpu.roll(head, back, axis=0)[SUBLANES:, :], pltpu.roll(u, back, axis=0)[SUBLANES:, :]],
                    axis=0)
                y = y + cw[FF_CONV - 1 - back:FF_CONV - back, :] * prev
            uv, ug = y[:, :LANES], y[:, LANES:]
            act_scr[rows, co:co + LANES] = (ug * _sigmoid(ug) * uv).astype(BF16)
        o_ref[rows, :] = x1_ref[rows, :] + _dot(act_scr[rows, :], wd_ref[...])


def _interleave_val_gate(a):
    dff = a.shape[-1] // 2
    blocks = []
    for c in range(dff // LANES):
        blocks.append(a[..., c * LANES:(c + 1) * LANES])
        blocks.append(a[..., dff + c * LANES:dff + (c + 1) * LANES])
    return jnp.concatenate(blocks, axis=-1)


def _ffn(x1, h2, w_up, conv_w, conv_b, w_down, seq, tm):
    t, d = x1.shape
    dff = w_down.shape[0]

    def resident(shape):
        return pl.BlockSpec(shape, lambda i: (0, 0), pipeline_mode=pl.Buffered(1))

    return pl.pallas_call(
        functools.partial(_ffn_kernel, tm=tm, tf=dff, tiles_per_seq=seq // tm),
        grid=(t // tm,),
        in_specs=[
            pl.BlockSpec((tm, d), lambda i: (i, 0)),
            pl.BlockSpec((tm, d), lambda i: (i, 0)),
            resident((d, 2 * dff)),
            resident((FF_CONV, 2 * dff)),
            resident((1, 2 * dff)),
            resident((dff, d)),
        ],
        out_specs=pl.BlockSpec((tm, d), lambda i: (i, 0)),
        out_shape=jax.ShapeDtypeStruct((t, d), F32),
        scratch_shapes=[
            pltpu.VMEM((SUBLANES, 2 * dff), F32),
            pltpu.VMEM((tm, dff), BF16),
        ],
        compiler_params=_cparams(("arbitrary",)),
        name="ffn",
    )(x1, h2, _interleave_val_gate(w_up), _interleave_val_gate(conv_w), _interleave_val_gate(conv_b), w_down)


def kernel(x, mem, g_mix, w_in, ml_conv_w, ml_conv_b, ml_b_i, ml_b_f, ml_g_out, g_mem,
           w_mem_kv, xa_g_q, xa_g_k, w_sb_out, w_ml_out, w_xa_out, w_o, g_ffn, w_up,
           ff_conv_w, ff_conv_b, w_down):
    bsz, seq, d = x.shape
    n_mem = mem.shape[1]
    depth = w_in.shape[0]
    sbw = SB_HEADS * SB_HEAD_DIM
    mlw = ML_HEADS * ML_HEAD_DIM
    xaw = XA_HEADS * XA_HEAD_DIM
    n_wide = 3 * sbw + 4 * mlw
    if_lo, if_hi = n_wide, n_wide + 2 * ML_HEADS
    sb_scale = math.log2(math.e) / math.sqrt(SB_HEAD_DIM)

    x2 = x.reshape(bsz * seq, d)
    mem2 = mem.reshape(bsz * n_mem, d)
    for l in range(depth):
        wt = w_in[l].T
        w_main = jnp.concatenate(
            [wt[:sbw] * (-sb_scale), wt[sbw:n_wide], wt[if_hi:if_hi + xaw]], axis=0).astype(BF16)
        w_gate = wt[if_hi + xaw:].astype(BF16)
        w_ifc = jnp.pad(wt[if_lo:if_hi], ((0, LANES - 2 * ML_HEADS), (0, 0))).astype(BF16)
        proj, ifc, ifr = _inproj(x2, g_mix[l][None, :], w_main, w_ifc, tm=512, tn=2048)

        ysb = _sb_attention(proj, bsz, seq, qcol=0, kcol=sbw // LANES, vcol=2 * sbw // LANES, tq=512)

        bias = jnp.concatenate([ml_b_i[l], ml_b_f[l]])
        bias_col = jnp.pad(bias, (0, LANES - 2 * ML_HEADS))[None, :]
        bias_row = bias[:, None]
        ml0 = 3 * sbw // mlw
        yml = _mlstm(proj, ifc, ifr, ml_conv_w[l], ml_conv_b[l][None, :], bias_col, bias_row,
                     ml_g_out[l].reshape(1, mlw), bsz, seq, cols=(ml0, ml0 + 1, ml0 + 2, ml0 + 3), nb=ML_BATCH)

        kn, vv = _memkv(mem2, g_mem[l][None, :], w_mem_kv[l].astype(BF16), xa_g_k[l][None, :], bsz, n_mem)
        x1, h2 = _merge(x2, ysb, yml, proj, n_wide // xaw, kn, vv, xa_g_q[l][None, :],
                        g_mix[l][None, :], w_gate,
                        w_sb_out[l].astype(BF16), w_ml_out[l].astype(BF16), w_xa_out[l].astype(BF16),
                        w_o[l].astype(BF16), g_ffn[l][None, :], seq, tm=512)

        x2 = _ffn(x1, h2, w_up[l].astype(BF16), ff_conv_w[l], ff_conv_b[l][None, :],
                  w_down[l].astype(BF16), seq, tm=512)
    return x2.reshape(bsz, seq, d)
```

```python
import functools
import math

import jax
import jax.numpy as jnp
from jax import lax
from jax.experimental import pallas as pl
from jax.experimental.pallas import tpu as pltpu

F32 = jnp.float32
BF16 = jnp.bfloat16

SB_HEADS = 8
SB_HEAD_DIM = 64
ML_HEADS = 4
ML_HEAD_DIM = 128
ML_CONV = 4
ML_CHUNK = 128
XA_HEADS = 4
XA_HEAD_DIM = 128
FF_CONV = 3
EPS = 1e-6
SB_UNROLL = 2
SB_DEAD_LOG2 = -150.0
ML_BATCH = 1
FFN_ROW_SPLIT = 2

LANES = 128
SUBLANES = 8
VMEM_LIMIT = 56 * 1024 * 1024


def _cparams(sem):
    return pltpu.CompilerParams(dimension_semantics=sem, vmem_limit_bytes=VMEM_LIMIT)


def _dot(a, b):
    return jnp.dot(a, b, preferred_element_type=F32)


def _dot_nt(a, b):
    return lax.dot_general(a, b, (((1,), (1,)), ((), ())), preferred_element_type=F32)


def _dot_tn(a, b):
    return lax.dot_general(a, b, (((0,), (0,)), ((), ())), preferred_element_type=F32)


def _sigmoid(x):
    return 0.5 * jnp.tanh(0.5 * x) + 0.5


def _inproj_kernel(x_ref, g_ref, w_ref, wif_ref, o_ref, ifc_ref, ifr_ref, h_scr, *, tn):
    j = pl.program_id(1)

    @pl.when(j == 0)
    def _():
        x = x_ref[...]
        ms = jnp.mean(x * x, axis=-1, keepdims=True)
        h = (x * lax.rsqrt(ms + EPS) * g_ref[...]).astype(BF16)
        h_scr[...] = h
        ifc = _dot_nt(h, wif_ref[...])
        ifc_ref[...] = ifc
        ifr_ref[...] = ifc.T[0:SUBLANES, :]

    o_ref[...] = _dot_nt(h_scr[...], w_ref[pl.ds(pl.multiple_of(j * tn, LANES), tn), :]).astype(BF16)


def _inproj(x2, g, w_main, w_if, tm, tn):
    t, d = x2.shape
    n = w_main.shape[0]
    return pl.pallas_call(
        functools.partial(_inproj_kernel, tn=tn),
        grid=(t // tm, n // tn),
        in_specs=[
            pl.BlockSpec((tm, d), lambda i, j: (i, 0)),
            pl.BlockSpec((1, d), lambda i, j: (0, 0)),
            pl.BlockSpec((n, d), lambda i, j: (0, 0), pipeline_mode=pl.Buffered(1)),
            pl.BlockSpec((LANES, d), lambda i, j: (0, 0)),
        ],
        out_specs=[
            pl.BlockSpec((tm, tn), lambda i, j: (i, j)),
            pl.BlockSpec((tm, LANES), lambda i, j: (i, 0)),
            pl.BlockSpec((SUBLANES, tm), lambda i, j: (0, i)),
        ],
        out_shape=[
            jax.ShapeDtypeStruct((t, n), BF16),
            jax.ShapeDtypeStruct((t, LANES), F32),
            jax.ShapeDtypeStruct((SUBLANES, t), F32),
        ],
        scratch_shapes=[pltpu.VMEM((tm, d), BF16)],
        compiler_params=_cparams(("arbitrary", "arbitrary")),
        name="inproj",
    )(x2, g, w_main, w_if)


def _sb_kernel(q_ref, k_ref, v_ref, o_ref, kk_scr, vv_scr, acc_scr, carry_scr, w_scr, *, tq):
    i = pl.program_id(2)
    tk = LANES
    nsub = tq // tk
    seq = k_ref.shape[0]

    @pl.when(i == 0)
    def _():
        lane = lax.broadcasted_iota(jnp.int32, (seq, LANES), 1)
        for src, dst in ((k_ref, kk_scr), (v_ref, vv_scr)):
            a = src[...]
            zero = jnp.zeros_like(a)
            dst[:, 0:tk, :] = jnp.where(lane < SB_HEAD_DIM, a, zero).reshape(seq // tk, tk, LANES)
            dst[:, tk:2 * tk, :] = jnp.where(lane >= SB_HEAD_DIM, a, zero).reshape(seq // tk, tk, LANES)

    r2 = lax.broadcasted_iota(jnp.int32, (2 * tk, 2 * tk), 0)
    c2 = lax.broadcasted_iota(jnp.int32, (2 * tk, 2 * tk), 1)
    suffix = jnp.where(((r2 // tk) == (c2 // tk)) & ((r2 % tk) > (c2 % tk)), 1.0, 0.0).astype(BF16)

    acc_scr[...] = jnp.zeros_like(acc_scr)
    carry_scr[...] = jnp.zeros_like(carry_scr)

    def step(kb_lo, nblk, tile_off):
        q = q_ref[...]
        kk = kk_scr[pl.ds(kb_lo, nblk)].reshape(nblk * 2 * tk, LANES)
        y = _dot_nt(q, kk)
        for b in reversed(range(nblk)):
            sl = slice(b * 2 * tk, (b + 1) * 2 * tk)
            kb = -1 if tile_off is None else tile_off + b
            r0 = max(kb, 0) * tk
            n = tq - r0
            yb = y[r0:, sl]
            neg_abs = lax.bitcast_convert_type(
                lax.bitcast_convert_type(yb, jnp.uint32) | jnp.uint32(0x80000000), F32)
            lf = jnp.minimum(yb, 0.0) - jnp.log2(1.0 + jnp.exp2(neg_abs))
            if kb >= 0:
                strict = (lax.broadcasted_iota(jnp.int32, (n, 2 * tk), 1) % tk
                          < lax.broadcasted_iota(jnp.int32, (n, 2 * tk), 0))
                lf = jnp.where(strict, lf, 0.0)
            suf = _dot(lf.astype(BF16), suffix)
            carry = carry_scr[r0:tq, :]
            w = jnp.exp2((lf - yb) + (suf + carry))
            if kb >= 0:
                w = jnp.where(strict, w, 0.0)
            w_scr[r0:tq, sl] = w.astype(BF16)
            if r0 > 0:
                w_scr[0:r0, sl] = jnp.zeros((r0, 2 * tk), BF16)
            tot0 = jnp.sum(lf[:, :tk], axis=1, keepdims=True)
            tot1 = jnp.sum(lf[:, tk:], axis=1, keepdims=True)
            carry_scr[r0:tq, :] = carry + jnp.concatenate(
                [jnp.broadcast_to(tot0, (n, tk)), jnp.broadcast_to(tot1, (n, tk))], axis=1)
        vv = vv_scr[pl.ds(kb_lo, nblk)].reshape(nblk * 2 * tk, LANES)
        acc_scr[...] += _dot(w_scr[:, 0:nblk * 2 * tk], vv)

    @pl.when(i == 0)
    def _():
        step(0, nsub, 0)

    @pl.when(i > 0)
    def _():
        step(i * nsub - SB_UNROLL, nsub + SB_UNROLL, -SB_UNROLL)

    n_steps = jnp.maximum(i * nsub // SB_UNROLL - 1, 0)

    def cond(state):
        it, live = state
        return jnp.logical_and(it < n_steps, live)

    def body(state):
        it, _ = state
        step(i * nsub - SB_UNROLL * (it + 2), SB_UNROLL, None)
        return it + 1, jnp.max(carry_scr[...]) >= SB_DEAD_LOG2

    lax.while_loop(cond, body, (jnp.int32(0), jnp.max(carry_scr[...]) >= SB_DEAD_LOG2))
    o_ref[...] = acc_scr[...].astype(BF16)


def _sb_attention(proj, bsz, seq, qcol, kcol, vcol, tq):
    npair = SB_HEADS * SB_HEAD_DIM // LANES
    nq = seq // tq
    return pl.pallas_call(
        functools.partial(_sb_kernel, tq=tq),
        grid=(bsz, npair, nq),
        in_specs=[
            pl.BlockSpec((tq, LANES), lambda b, p, i: (b * nq + i, qcol + p)),
            pl.BlockSpec((seq, LANES), lambda b, p, i: (b, kcol + p)),
            pl.BlockSpec((seq, LANES), lambda b, p, i: (b, vcol + p)),
        ],
        out_specs=pl.BlockSpec((tq, LANES), lambda b, p, i: (b * nq + i, p)),
        out_shape=jax.ShapeDtypeStruct((bsz * seq, npair * LANES), BF16),
        scratch_shapes=[
            pltpu.VMEM((seq // LANES, 2 * LANES, LANES), BF16),
            pltpu.VMEM((seq // LANES, 2 * LANES, LANES), BF16),
            pltpu.VMEM((tq, LANES), F32),
            pltpu.VMEM((tq, 2 * LANES), F32),
            pltpu.VMEM((tq, (tq // LANES + SB_UNROLL) * 2 * LANES), BF16),
        ],
        compiler_params=_cparams(("arbitrary", "arbitrary", "arbitrary")),
        name="sb_attn",
    )(proj, proj, proj)


def _mlstm_kernel(*refs, nb):
    q_ref, k_ref, v_ref, o_ref, ifc_ref = refs[:5]
    ifr_refs = refs[5:5 + nb]
    cw_ref, cb_ref, bic_ref, bir_ref, gout_ref, y_ref, qtail, ktail, ct_scr, m_scr = refs[5 + nb:]
    c = pl.program_id(1)
    L = ML_CHUNK
    W = ML_HEADS * ML_HEAD_DIM

    @pl.when(c == 0)
    def _():
        qtail[...] = jnp.zeros_like(qtail)
        ktail[...] = jnp.zeros_like(ktail)
        ct_scr[...] = jnp.zeros_like(ct_scr)
        m_scr[...] = jnp.zeros_like(m_scr)

    stores = []
    m_all = m_scr[...]
    ct_all = [ct_scr[st] for st in range(nb * ML_HEADS)]
    qtails = [qtail[r] for r in range(nb)]
    ktails = [ktail[r] for r in range(nb)]

    def conv_silu(x, tail, tail_ref, r, w, b):
        head = jnp.concatenate([tail, x[0:SUBLANES, :]], axis=0)
        stores.append((tail_ref, r, x[L - SUBLANES:L, :]))
        y = b + w[ML_CONV - 1:ML_CONV, :] * x
        for back in range(1, ML_CONV):
            prev = jnp.concatenate(
                [pltpu.roll(head, back, axis=0)[SUBLANES:, :], pltpu.roll(x, back, axis=0)[SUBLANES:, :]],
                axis=0)
            y = y + w[ML_CONV - 1 - back:ML_CONV - back, :] * prev
        return y * _sigmoid(y)

    cw = cw_ref[...]
    cb = cb_ref[...]
    row = lax.broadcasted_iota(jnp.int32, (L, L), 0)
    col = lax.broadcasted_iota(jnp.int32, (L, L), 1)
    causal = col <= row
    ones = jnp.ones((L, LANES), BF16)

    for r in range(nb):
        qc = conv_silu(q_ref[r].astype(F32), qtails[r], qtail, r, cw[:, :W], cb[:, :W])
        kc = conv_silu(k_ref[r].astype(F32), ktails[r], ktail, r, cw[:, W:], cb[:, W:])
        kc = kc * (1.0 / math.sqrt(ML_HEAD_DIM))
        ifc = ifc_ref[r] + bic_ref[...]
        ifr = ifr_refs[r][...] + bir_ref[...]

        for h in range(ML_HEADS):
            st = r * ML_HEADS + h
            sl = slice(h * ML_HEAD_DIM, (h + 1) * ML_HEAD_DIM)
            i_col = ifc[:, h:h + 1]
            f_col = ifc[:, ML_HEADS + h:ML_HEADS + h + 1]
            i_row = ifr[h:h + 1, :]
            f_row = ifr[ML_HEADS + h:ML_HEADS + h + 1, :]
            lf_col = jnp.minimum(f_col, 0.0) - jnp.log(1.0 + jnp.exp(-jnp.abs(f_col)))
            lf_row = jnp.minimum(f_row, 0.0) - jnp.log(1.0 + jnp.exp(-jnp.abs(f_row)))
            b_col = jnp.sum(jnp.where(causal, lf_row, 0.0), axis=1, keepdims=True)
            b_row = jnp.sum(jnp.where(row <= col, lf_col, 0.0), axis=0, keepdims=True)
            b_last = jnp.sum(lf_row, axis=1, keepdims=True)
            m_st = m_all[st, 0:1, 0:1]
            log_d = jnp.where(causal, b_col - b_row + i_row, -jnp.inf)
            m_inter = b_col + m_st
            m_t = jnp.maximum(m_inter, jnp.max(log_d, axis=1, keepdims=True))
            q_f = qc[:, sl]
            k_h = kc[:, sl]
            v_ext = jnp.concatenate([v_ref[r, :, sl], ones], axis=1)
            s = _dot_nt(q_f.astype(BF16), k_h.astype(BF16)) * jnp.exp(log_d - m_t)
            w_inter = jnp.exp(m_inter - m_t)
            ct = ct_all[st]
            num = _dot(jnp.concatenate([s.astype(BF16), (w_inter * q_f).astype(BF16)], axis=1),
                       jnp.concatenate([v_ext, ct.astype(BF16)], axis=0))
            den = num[:, ML_HEAD_DIM:]
            hh = num[:, :ML_HEAD_DIM] / jnp.maximum(jnp.abs(den), jnp.exp(-m_t))

            log_g = b_last - b_col + i_col
            m_new = jnp.maximum(b_last + m_st, jnp.max(log_g, axis=0, keepdims=True))
            decay = jnp.exp(b_last + m_st - m_new)
            wk = jnp.exp(log_g - m_new)
            stores.append((ct_scr, st, decay * ct + _dot_tn((wk * k_h).astype(BF16), v_ext)))
            stores.append((m_scr, st, jnp.broadcast_to(m_new, (SUBLANES, LANES))))

            hn = hh * lax.rsqrt(jnp.mean(hh * hh, axis=-1, keepdims=True) + EPS) * gout_ref[:, sl]
            stores.append((y_ref, (r, slice(None), sl), (_sigmoid(o_ref[r, :, sl].astype(F32)) * hn).astype(BF16)))

    for ref, idx, val in stores:
        ref[idx] = val


def _mlstm(proj, ifc, ifr, conv_w, conv_b, bias_col, bias_row, g_out, bsz, seq, cols, nb):
    qcol, kcol, vcol, ocol = cols
    L = ML_CHUNK
    W = ML_HEADS * ML_HEAD_DIM
    nc = seq // L
    proj3 = proj.reshape(bsz, seq, proj.shape[1])
    ifc3 = ifc.reshape(bsz, seq, LANES)

    def colspec(cb):
        return pl.BlockSpec((nb, L, W), lambda b, c: (b, c, cb))

    def const(shape):
        return pl.BlockSpec(shape, lambda b, c: (0,) * len(shape))

    ifr_specs = [pl.BlockSpec((SUBLANES, L), lambda b, c, r=r: (0, (b * nb + r) * nc + c)) for r in range(nb)]
    y = pl.pallas_call(
        functools.partial(_mlstm_kernel, nb=nb),
        grid=(bsz // nb, nc),
        in_specs=[
            colspec(qcol), colspec(kcol), colspec(vcol), colspec(ocol),
            pl.BlockSpec((nb, L, LANES), lambda b, c: (b, c, 0)),
            *ifr_specs,
            const((ML_CONV, 2 * W)), const((1, 2 * W)),
            const((1, LANES)), const((SUBLANES, 1)), const((1, W)),
        ],
        out_specs=pl.BlockSpec((nb, L, W), lambda b, c: (b, c, 0)),
        out_shape=jax.ShapeDtypeStruct((bsz, seq, W), BF16),
        scratch_shapes=[
            pltpu.VMEM((nb, SUBLANES, W), F32),
            pltpu.VMEM((nb, SUBLANES, W), F32),
            pltpu.VMEM((nb * ML_HEADS, ML_HEAD_DIM, ML_HEAD_DIM + LANES), F32),
            pltpu.VMEM((nb * ML_HEADS, SUBLANES, LANES), F32),
        ],
        compiler_params=_cparams(("arbitrary", "arbitrary")),
        name="mlstm",
    )(proj3, proj3, proj3, proj3, ifc3, *([ifr] * nb), conv_w, conv_b, bias_col, bias_row, g_out)
    return y.reshape(bsz * seq, W)


def _memkv_kernel(mem_ref, g_ref, w_ref, gk_ref, k_ref, v_ref):
    x = mem_ref[...]
    ms = jnp.mean(x * x, axis=-1, keepdims=True)
    mn = (x * lax.rsqrt(ms + EPS) * g_ref[...]).astype(BF16)
    kv = _dot(mn, w_ref[...])
    W = XA_HEADS * XA_HEAD_DIM
    for h in range(XA_HEADS):
        sl = slice(h * XA_HEAD_DIM, (h + 1) * XA_HEAD_DIM)
        kh = kv[:, sl]
        kn = kh * lax.rsqrt(jnp.mean(kh * kh, axis=-1, keepdims=True) + EPS) * gk_ref[...]
        k_ref[:, sl] = kn.astype(BF16)
    v_ref[...] = kv[:, W:].astype(BF16)


def _memkv(mem2, g_mem, w_kv, g_k, bsz, n_mem):
    d = mem2.shape[1]
    W = XA_HEADS * XA_HEAD_DIM
    return pl.pallas_call(
        _memkv_kernel,
        grid=(bsz,),
        in_specs=[
            pl.BlockSpec((n_mem, d), lambda b: (b, 0)),
            pl.BlockSpec((1, d), lambda b: (0, 0)),
            pl.BlockSpec((d, 2 * W), lambda b: (0, 0)),
            pl.BlockSpec((1, XA_HEAD_DIM), lambda b: (0, 0)),
        ],
        out_specs=[pl.BlockSpec((n_mem, W), lambda b: (b, 0)),
                   pl.BlockSpec((n_mem, W), lambda b: (b, 0))],
        out_shape=[jax.ShapeDtypeStruct((bsz * n_mem, W), BF16)] * 2,
        compiler_params=_cparams(("arbitrary",)),
        name="memkv",
    )(mem2, g_mem, w_kv, g_k)


def _merge_kernel(x_ref, ysb_ref, yml_ref, xq_ref, kn_ref, vv_ref, gq_ref, gm_ref, wg_ref,
                  wsb_ref, wml_ref, wxa_ref, wo_ref, gf_ref, x1_ref, h2_ref):
    scale = 1.0 / math.sqrt(XA_HEAD_DIM)
    heads = []
    for hd in range(XA_HEADS):
        sl = slice(hd * XA_HEAD_DIM, (hd + 1) * XA_HEAD_DIM)
        qh = xq_ref[:, sl].astype(F32)
        qn = qh * lax.rsqrt(jnp.mean(qh * qh, axis=-1, keepdims=True) + EPS) * gq_ref[...]
        s = _dot_nt(qn.astype(BF16), kn_ref[:, sl]) * scale
        p = jnp.exp(s - jnp.max(s, axis=-1, keepdims=True))
        p = p / jnp.sum(p, axis=-1, keepdims=True)
        heads.append(_dot(p.astype(BF16), vv_ref[:, sl]).astype(BF16))
    yxa = jnp.concatenate(heads, axis=1)

    x = x_ref[...]
    d = x.shape[1]
    h = (x * lax.rsqrt(jnp.mean(x * x, axis=-1, keepdims=True) + EPS) * gm_ref[...]).astype(BF16)
    merged = None
    for br, (y, w_ref) in enumerate(((ysb_ref[...], wsb_ref), (yml_ref[...], wml_ref), (yxa, wxa_ref))):
        gate = _sigmoid(_dot_nt(h, wg_ref[br * d:(br + 1) * d, :]))
        term = gate * _dot(y, w_ref[...])
        merged = term if merged is None else merged + term
    x1 = x + _dot(merged.astype(BF16), wo_ref[...])
    x1_ref[...] = x1
    ms = jnp.mean(x1 * x1, axis=-1, keepdims=True)
    h2_ref[...] = (x1 * lax.rsqrt(ms + EPS) * gf_ref[...]).astype(BF16)


def _merge(x2, ysb, yml, proj, xq_col, kn, vv, g_q, g_mix, w_gate, w_sb, w_ml, w_xa, w_o, g_ffn, seq, tm):
    t, d = x2.shape
    wb = ysb.shape[1]
    n_mem = kn.shape[0] // (t // seq)
    tiles_per_seq = seq // tm

    def rows(width, cb=0):
        return pl.BlockSpec((tm, width), lambda i: (i, cb))

    def const(shape):
        return pl.BlockSpec(shape, lambda i: (0, 0), pipeline_mode=pl.Buffered(1))

    def per_batch(shape):
        return pl.BlockSpec(shape, lambda i: (i // tiles_per_seq, 0))

    return pl.pallas_call(
        _merge_kernel,
        grid=(t // tm,),
        in_specs=[
            rows(d), rows(wb), rows(wb), rows(wb, xq_col),
            per_batch((n_mem, wb)), per_batch((n_mem, wb)), const((1, XA_HEAD_DIM)),
            const((1, d)), const((3 * d, d)),
            const((wb, d)), const((wb, d)), const((wb, d)), const((d, d)), const((1, d)),
        ],
        out_specs=[rows(d), rows(d)],
        out_shape=[jax.ShapeDtypeStruct((t, d), F32), jax.ShapeDtypeStruct((t, d), BF16)],
        compiler_params=_cparams(("arbitrary",)),
        name="merge",
    )(x2, ysb, yml, proj, kn, vv, g_q, g_mix, w_gate, w_sb, w_ml, w_xa, w_o, g_ffn)


def _ffn_kernel(x1_ref, h2_ref, wup_ref, cw_ref, cb_ref, wd_ref, o_ref, tail_scr, act_scr,
                *, tm, tf, tiles_per_seq):
    i = pl.program_id(0)
    first = (i % tiles_per_seq) == 0
    th = tm // FFN_ROW_SPLIT

    for hh in range(FFN_ROW_SPLIT):
        rows = slice(hh * th, (hh + 1) * th)
        h2 = h2_ref[rows, :]
        for co in range(0, tf, LANES):
            cols = slice(2 * co, 2 * co + 2 * LANES)
            u = _dot(h2, wup_ref[:, cols])
            tail = tail_scr[:, cols]
            if hh == 0:
                tail = jnp.where(first, jnp.zeros_like(tail), tail)
            tail_scr[:, cols] = u[th - SUBLANES:th, :]
            head = jnp.concatenate([tail, u[0:SUBLANES, :]], axis=0)
            cw = cw_ref[:, cols]
            y = cb_ref[:, cols] + cw[FF_CONV - 1:FF_CONV, :] * u
            for back in range(1, FF_CONV):
                prev = jnp.concatenate(
                    [pltpu.roll(head, back, axis=0)[SUBLANES:, :], pltpu.roll(u, back, axis=0)[SUBLANES:, :]],
                    axis=0)
                y = y + cw[FF_CONV - 1 - back:FF_CONV - back, :] * prev
            uv, ug = y[:, :LANES], y[:, LANES:]
            act_scr[rows, co:co + LANES] = (ug * _sigmoid(ug) * uv).astype(BF16)
        o_ref[rows, :] = x1_ref[rows, :] + _dot(act_scr[rows, :], wd_ref[...])


def _interleave_val_gate(a):
    dff = a.shape[-1] // 2
    blocks = []
    for c in range(dff // LANES):
        blocks.append(a[..., c * LANES:(c + 1) * LANES])
        blocks.append(a[..., dff + c * LANES:dff + (c + 1) * LANES])
    return jnp.concatenate(blocks, axis=-1)


def _ffn(x1, h2, w_up, conv_w, conv_b, w_down, seq, tm):
    t, d = x1.shape
    dff = w_down.shape[0]

    def resident(shape):
        return pl.BlockSpec(shape, lambda i: (0, 0), pipeline_mode=pl.Buffered(1))

    return pl.pallas_call(
        functools.partial(_ffn_kernel, tm=tm, tf=dff, tiles_per_seq=seq // tm),
        grid=(t // tm,),
        in_specs=[
            pl.BlockSpec((tm, d), lambda i: (i, 0)),
            pl.BlockSpec((tm, d), lambda i: (i, 0)),
            resident((d, 2 * dff)),
            resident((FF_CONV, 2 * dff)),
            resident((1, 2 * dff)),
            resident((dff, d)),
        ],
        out_specs=pl.BlockSpec((tm, d), lambda i: (i, 0)),
        out_shape=jax.ShapeDtypeStruct((t, d), F32),
        scratch_shapes=[
            pltpu.VMEM((SUBLANES, 2 * dff), F32),
            pltpu.VMEM((tm, dff), BF16),
        ],
        compiler_params=_cparams(("arbitrary",)),
        name="ffn",
    )(x1, h2, _interleave_val_gate(w_up), _interleave_val_gate(conv_w), _interleave_val_gate(conv_b), w_down)


def kernel(x, mem, g_mix, w_in, ml_conv_w, ml_conv_b, ml_b_i, ml_b_f, ml_g_out, g_mem,
           w_mem_kv, xa_g_q, xa_g_k, w_sb_out, w_ml_out, w_xa_out, w_o, g_ffn, w_up,
           ff_conv_w, ff_conv_b, w_down):
    bsz, seq, d = x.shape
    n_mem = mem.shape[1]
    depth = w_in.shape[0]
    sbw = SB_HEADS * SB_HEAD_DIM
    mlw = ML_HEADS * ML_HEAD_DIM
    xaw = XA_HEADS * XA_HEAD_DIM
    n_wide = 3 * sbw + 4 * mlw
    if_lo, if_hi = n_wide, n_wide + 2 * ML_HEADS
    sb_scale = math.log2(math.e) / math.sqrt(SB_HEAD_DIM)

    x2 = x.reshape(bsz * seq, d)
    mem2 = mem.reshape(bsz * n_mem, d)
    for l in range(depth):
        wt = w_in[l].T
        w_main = jnp.concatenate(
            [wt[:sbw] * (-sb_scale), wt[sbw:n_wide], wt[if_hi:if_hi + xaw]], axis=0).astype(BF16)
        w_gate = wt[if_hi + xaw:].astype(BF16)
        w_ifc = jnp.pad(wt[if_lo:if_hi], ((0, LANES - 2 * ML_HEADS), (0, 0))).astype(BF16)
        proj, ifc, ifr = _inproj(x2, g_mix[l][None, :], w_main, w_ifc, tm=512, tn=4096)

        ysb = _sb_attention(proj, bsz, seq, qcol=0, kcol=sbw // LANES, vcol=2 * sbw // LANES, tq=512)

        bias = jnp.concatenate([ml_b_i[l], ml_b_f[l]])
        bias_col = jnp.pad(bias, (0, LANES - 2 * ML_HEADS))[None, :]
        bias_row = bias[:, None]
        ml0 = 3 * sbw // mlw
        yml = _mlstm(proj, ifc, ifr, ml_conv_w[l], ml_conv_b[l][None, :], bias_col, bias_row,
                     ml_g_out[l].reshape(1, mlw), bsz, seq, cols=(ml0, ml0 + 1, ml0 + 2, ml0 + 3), nb=ML_BATCH)

        kn, vv = _memkv(mem2, g_mem[l][None, :], w_mem_kv[l].astype(BF16), xa_g_k[l][None, :], bsz, n_mem)
        x1, h2 = _merge(x2, ysb, yml, proj, n_wide // xaw, kn, vv, xa_g_q[l][None, :],
                        g_mix[l][None, :], w_gate,
                        w_sb_out[l].astype(BF16), w_ml_out[l].astype(BF16), w_xa_out[l].astype(BF16),
                        w_o[l].astype(BF16), g_ffn[l][None, :], seq, tm=512)

        x2 = _ffn(x1, h2, w_up[l].astype(BF16), ff_conv_w[l], ff_conv_b[l][None, :],
                  w_down[l].astype(BF16), seq, tm=512)
    return x2.reshape(bsz, seq, d)
```

```python
import functools
import math

import jax
import jax.numpy as jnp
from jax import lax
from jax.experimental import pallas as pl
from jax.experimental.pallas import tpu as pltpu

F32 = jnp.float32
BF16 = jnp.bfloat16

SB_HEADS = 8
SB_HEAD_DIM = 64
ML_HEADS = 4
ML_HEAD_DIM = 128
ML_CONV = 4
ML_CHUNK = 128
XA_HEADS = 4
XA_HEAD_DIM = 128
FF_CONV = 3
EPS = 1e-6
SB_UNROLL = 2
SB_TILES_PER_STEP = 2
SB_DEAD_LOG2 = -150.0
ML_BATCH = 1
FFN_ROW_SPLIT = 2

LANES = 128
SUBLANES = 8
V7X_VMEM_BYTES = 64 * 1024 * 1024
VMEM_LIMIT = V7X_VMEM_BYTES * 7 // 8


def _cparams(sem):
    return pltpu.CompilerParams(dimension_semantics=sem, vmem_limit_bytes=VMEM_LIMIT)


def _dot(a, b):
    return jnp.dot(a, b, preferred_element_type=F32)


def _dot_nt(a, b):
    return lax.dot_general(a, b, (((1,), (1,)), ((), ())), preferred_element_type=F32)


def _dot_tn(a, b):
    return lax.dot_general(a, b, (((0,), (0,)), ((), ())), preferred_element_type=F32)


def _sigmoid(x):
    return 0.5 * jnp.tanh(0.5 * x) + 0.5


def _inproj_kernel(x_ref, g_ref, w_ref, wif_ref, o_ref, ifc_ref, ifr_ref, h_scr, *, tn):
    j = pl.program_id(1)

    @pl.when(j == 0)
    def _():
        x = x_ref[...]
        ms = jnp.mean(x * x, axis=-1, keepdims=True)
        h = (x * lax.rsqrt(ms + EPS) * g_ref[...]).astype(BF16)
        h_scr[...] = h
        ifc = _dot_nt(h, wif_ref[...])
        ifc_ref[...] = ifc
        ifr_ref[...] = ifc.T[0:SUBLANES, :]

    o_ref[...] = _dot_nt(h_scr[...], w_ref[pl.ds(pl.multiple_of(j * tn, LANES), tn), :]).astype(BF16)


def _inproj(x2, g, w_main, w_if, tm, tn):
    t, d = x2.shape
    n = w_main.shape[0]
    return pl.pallas_call(
        functools.partial(_inproj_kernel, tn=tn),
        grid=(t // tm, n // tn),
        in_specs=[
            pl.BlockSpec((tm, d), lambda i, j: (i, 0)),
            pl.BlockSpec((1, d), lambda i, j: (0, 0)),
            pl.BlockSpec((n, d), lambda i, j: (0, 0), pipeline_mode=pl.Buffered(1)),
            pl.BlockSpec((LANES, d), lambda i, j: (0, 0)),
        ],
        out_specs=[
            pl.BlockSpec((tm, tn), lambda i, j: (i, j)),
            pl.BlockSpec((tm, LANES), lambda i, j: (i, 0)),
            pl.BlockSpec((SUBLANES, tm), lambda i, j: (0, i)),
        ],
        out_shape=[
            jax.ShapeDtypeStruct((t, n), BF16),
            jax.ShapeDtypeStruct((t, LANES), F32),
            jax.ShapeDtypeStruct((SUBLANES, t), F32),
        ],
        scratch_shapes=[pltpu.VMEM((tm, d), BF16)],
        compiler_params=_cparams(("arbitrary", "arbitrary")),
        name="inproj",
    )(x2, g, w_main, w_if)


def _sb_kernel(q_ref, k_ref, v_ref, o_ref, kk_scr, vv_scr, acc_scr, carry_scr, w_scr, *, tq):
    i = pl.program_id(2)
    tk = LANES
    nsub = tq // tk
    seq = k_ref.shape[0]

    @pl.when(i == 0)
    def _():
        lane = lax.broadcasted_iota(jnp.int32, (seq, LANES), 1)
        for src, dst in ((k_ref, kk_scr), (v_ref, vv_scr)):
            a = src[...]
            zero = jnp.zeros_like(a)
            dst[:, 0:tk, :] = jnp.where(lane < SB_HEAD_DIM, a, zero).reshape(seq // tk, tk, LANES)
            dst[:, tk:2 * tk, :] = jnp.where(lane >= SB_HEAD_DIM, a, zero).reshape(seq // tk, tk, LANES)

    r2 = lax.broadcasted_iota(jnp.int32, (2 * tk, 2 * tk), 0)
    c2 = lax.broadcasted_iota(jnp.int32, (2 * tk, 2 * tk), 1)
    suffix = jnp.where(((r2 // tk) == (c2 // tk)) & ((r2 % tk) > (c2 % tk)), 1.0, 0.0).astype(BF16)

    def tile(sub):
        t = i * SB_TILES_PER_STEP + sub
        ro = sub * tq
        acc_scr[...] = jnp.zeros_like(acc_scr)
        carry_scr[...] = jnp.zeros_like(carry_scr)

        def step(kb_lo, nblk, tile_off):
            q = q_ref[ro:ro + tq, :]
            kk = kk_scr[pl.ds(kb_lo, nblk)].reshape(nblk * 2 * tk, LANES)
            y = _dot_nt(q, kk)
            for b in reversed(range(nblk)):
                sl = slice(b * 2 * tk, (b + 1) * 2 * tk)
                kb = -1 if tile_off is None else tile_off + b
                r0 = max(kb, 0) * tk
                n = tq - r0
                yb = y[r0:, sl]
                neg_abs = lax.bitcast_convert_type(
                    lax.bitcast_convert_type(yb, jnp.uint32) | jnp.uint32(0x80000000), F32)
                lf = jnp.minimum(yb, 0.0) - jnp.log2(1.0 + jnp.exp2(neg_abs))
                if kb >= 0:
                    strict = (lax.broadcasted_iota(jnp.int32, (n, 2 * tk), 1) % tk
                              < lax.broadcasted_iota(jnp.int32, (n, 2 * tk), 0))
                    lf = jnp.where(strict, lf, 0.0)
                suf = _dot(lf.astype(BF16), suffix)
                carry = carry_scr[r0:tq, :]
                w = jnp.exp2((lf - yb) + (suf + carry))
                if kb >= 0:
                    w = jnp.where(strict, w, 0.0)
                w_scr[r0:tq, sl] = w.astype(BF16)
                if r0 > 0:
                    w_scr[0:r0, sl] = jnp.zeros((r0, 2 * tk), BF16)
                tot0 = jnp.sum(lf[:, :tk], axis=1, keepdims=True)
                tot1 = jnp.sum(lf[:, tk:], axis=1, keepdims=True)
                carry_scr[r0:tq, :] = carry + jnp.concatenate(
                    [jnp.broadcast_to(tot0, (n, tk)), jnp.broadcast_to(tot1, (n, tk))], axis=1)
            vv = vv_scr[pl.ds(kb_lo, nblk)].reshape(nblk * 2 * tk, LANES)
            acc_scr[...] += _dot(w_scr[:, 0:nblk * 2 * tk], vv)

        if sub == 0:
            @pl.when(t == 0)
            def _():
                step(0, nsub, 0)

            @pl.when(t > 0)
            def _():
                step(t * nsub - SB_UNROLL, nsub + SB_UNROLL, -SB_UNROLL)
        else:
            step(t * nsub - SB_UNROLL, nsub + SB_UNROLL, -SB_UNROLL)

        n_steps = jnp.maximum(t * nsub // SB_UNROLL - 1, 0)

        def cond(state):
            it, live = state
            return jnp.logical_and(it < n_steps, live)

        def body(state):
            it, _ = state
            step(t * nsub - SB_UNROLL * (it + 2), SB_UNROLL, None)
            return it + 1, jnp.max(carry_scr[...]) >= SB_DEAD_LOG2

        lax.while_loop(cond, body, (jnp.int32(0), jnp.max(carry_scr[...]) >= SB_DEAD_LOG2))
        o_ref[ro:ro + tq, :] = acc_scr[...].astype(BF16)

    for sub in range(SB_TILES_PER_STEP):
        tile(sub)


def _sb_attention(proj, bsz, seq, qcol, kcol, vcol, tq):
    npair = SB_HEADS * SB_HEAD_DIM // LANES
    rows = tq * SB_TILES_PER_STEP
    nq = seq // rows
    return pl.pallas_call(
        functools.partial(_sb_kernel, tq=tq),
        grid=(bsz, npair, nq),
        in_specs=[
            pl.BlockSpec((rows, LANES), lambda b, p, i: (b * nq + i, qcol + p)),
            pl.BlockSpec((seq, LANES), lambda b, p, i: (b, kcol + p)),
            pl.BlockSpec((seq, LANES), lambda b, p, i: (b, vcol + p)),
        ],
        out_specs=pl.BlockSpec((rows, LANES), lambda b, p, i: (b * nq + i, p)),
        out_shape=jax.ShapeDtypeStruct((bsz * seq, npair * LANES), BF16),
        scratch_shapes=[
            pltpu.VMEM((seq // LANES, 2 * LANES, LANES), BF16),
            pltpu.VMEM((seq // LANES, 2 * LANES, LANES), BF16),
            pltpu.VMEM((tq, LANES), F32),
            pltpu.VMEM((tq, 2 * LANES), F32),
            pltpu.VMEM((tq, (tq // LANES + SB_UNROLL) * 2 * LANES), BF16),
        ],
        compiler_params=_cparams(("arbitrary", "arbitrary", "arbitrary")),
        name="sb_attn",
    )(proj, proj, proj)


def _mlstm_kernel(*refs, nb):
    q_ref, k_ref, v_ref, o_ref, ifc_ref = refs[:5]
    ifr_refs = refs[5:5 + nb]
    cw_ref, cb_ref, bic_ref, bir_ref, gout_ref, y_ref, qtail, ktail, ct_scr, m_scr = refs[5 + nb:]
    c = pl.program_id(1)
    L = ML_CHUNK
    W = ML_HEADS * ML_HEAD_DIM

    @pl.when(c == 0)
    def _():
        qtail[...] = jnp.zeros_like(qtail)
        ktail[...] = jnp.zeros_like(ktail)
        ct_scr[...] = jnp.zeros_like(ct_scr)
        m_scr[...] = jnp.zeros_like(m_scr)

    stores = []
    m_all = m_scr[...]
    ct_all = [ct_scr[st] for st in range(nb * ML_HEADS)]
    qtails = [qtail[r] for r in range(nb)]
    ktails = [ktail[r] for r in range(nb)]

    def conv_silu(x, tail, tail_ref, r, w, b):
        head = jnp.concatenate([tail, x[0:SUBLANES, :]], axis=0)
        stores.append((tail_ref, r, x[L - SUBLANES:L, :]))
        y = b + w[ML_CONV - 1:ML_CONV, :] * x
        for back in range(1, ML_CONV):
            prev = jnp.concatenate(
                [pltpu.roll(head, back, axis=0)[SUBLANES:, :], pltpu.roll(x, back, axis=0)[SUBLANES:, :]],
                axis=0)
            y = y + w[ML_CONV - 1 - back:ML_CONV - back, :] * prev
        return y * _sigmoid(y)

    cw = cw_ref[...]
    cb = cb_ref[...]
    row = lax.broadcasted_iota(jnp.int32, (L, L), 0)
    col = lax.broadcasted_iota(jnp.int32, (L, L), 1)
    causal = col <= row
    ones = jnp.ones((L, LANES), BF16)

    for r in range(nb):
        qc = conv_silu(q_ref[r].astype(F32), qtails[r], qtail, r, cw[:, :W], cb[:, :W])
        kc = conv_silu(k_ref[r].astype(F32), ktails[r], ktail, r, cw[:, W:], cb[:, W:])
        kc = kc * (1.0 / math.sqrt(ML_HEAD_DIM))
        ifc = ifc_ref[r] + bic_ref[...]
        ifr = ifr_refs[r][...] + bir_ref[...]

        for h in range(ML_HEADS):
            st = r * ML_HEADS + h
            sl = slice(h * ML_HEAD_DIM, (h + 1) * ML_HEAD_DIM)
            i_col = ifc[:, h:h + 1]
            f_col = ifc[:, ML_HEADS + h:ML_HEADS + h + 1]
            i_row = ifr[h:h + 1, :]
            f_row = ifr[ML_HEADS + h:ML_HEADS + h + 1, :]
            lf_col = jnp.minimum(f_col, 0.0) - jnp.log(1.0 + jnp.exp(-jnp.abs(f_col)))
            lf_row = jnp.minimum(f_row, 0.0) - jnp.log(1.0 + jnp.exp(-jnp.abs(f_row)))
            b_col = jnp.sum(jnp.where(causal, lf_row, 0.0), axis=1, keepdims=True)
            b_row = jnp.sum(jnp.where(row <= col, lf_col, 0.0), axis=0, keepdims=True)
            b_last = jnp.sum(lf_row, axis=1, keepdims=True)
            m_st = m_all[st, 0:1, 0:1]
            log_d = jnp.where(causal, b_col - b_row + i_row, -jnp.inf)
            m_inter = b_col + m_st
            m_t = jnp.maximum(m_inter, jnp.max(log_d, axis=1, keepdims=True))
            q_f = qc[:, sl]
            k_h = kc[:, sl]
            v_ext = jnp.concatenate([v_ref[r, :, sl], ones], axis=1)
            s = _dot_nt(q_f.astype(BF16), k_h.astype(BF16)) * jnp.exp(log_d - m_t)
            w_inter = jnp.exp(m_inter - m_t)
            ct = ct_all[st]
            num = _dot(jnp.concatenate([s.astype(BF16), (w_inter * q_f).astype(BF16)], axis=1),
                       jnp.concatenate([v_ext, ct.astype(BF16)], axis=0))
            den = num[:, ML_HEAD_DIM:]
            hh = num[:, :ML_HEAD_DIM] / jnp.maximum(jnp.abs(den), jnp.exp(-m_t))

            log_g = b_last - b_col + i_col
            m_new = jnp.maximum(b_last + m_st, jnp.max(log_g, axis=0, keepdims=True))
            decay = jnp.exp(b_last + m_st - m_new)
            wk = jnp.exp(log_g - m_new)
            stores.append((ct_scr, st, decay * ct + _dot_tn((wk * k_h).astype(BF16), v_ext)))
            stores.append((m_scr, st, jnp.broadcast_to(m_new, (SUBLANES, LANES))))

            hn = hh * lax.rsqrt(jnp.mean(hh * hh, axis=-1, keepdims=True) + EPS) * gout_ref[:, sl]
            stores.append((y_ref, (r, slice(None), sl), (_sigmoid(o_ref[r, :, sl].astype(F32)) * hn).astype(BF16)))

    for ref, idx, val in stores:
        ref[idx] = val


def _mlstm(proj, ifc, ifr, conv_w, conv_b, bias_col, bias_row, g_out, bsz, seq, cols, nb):
    qcol, kcol, vcol, ocol = cols
    L = ML_CHUNK
    W = ML_HEADS * ML_HEAD_DIM
    nc = seq // L
    proj3 = proj.reshape(bsz, seq, proj.shape[1])
    ifc3 = ifc.reshape(bsz, seq, LANES)

    def colspec(cb):
        return pl.BlockSpec((nb, L, W), lambda b, c: (b, c, cb))

    def const(shape):
        return pl.BlockSpec(shape, lambda b, c: (0,) * len(shape))

    ifr_specs = [pl.BlockSpec((SUBLANES, L), lambda b, c, r=r: (0, (b * nb + r) * nc + c)) for r in range(nb)]
    y = pl.pallas_call(
        functools.partial(_mlstm_kernel, nb=nb),
        grid=(bsz // nb, nc),
        in_specs=[
            colspec(qcol), colspec(kcol), colspec(vcol), colspec(ocol),
            pl.BlockSpec((nb, L, LANES), lambda b, c: (b, c, 0)),
            *ifr_specs,
            const((ML_CONV, 2 * W)), const((1, 2 * W)),
            const((1, LANES)), const((SUBLANES, 1)), const((1, W)),
        ],
        out_specs=pl.BlockSpec((nb, L, W), lambda b, c: (b, c, 0)),
        out_shape=jax.ShapeDtypeStruct((bsz, seq, W), BF16),
        scratch_shapes=[
            pltpu.VMEM((nb, SUBLANES, W), F32),
            pltpu.VMEM((nb, SUBLANES, W), F32),
            pltpu.VMEM((nb * ML_HEADS, ML_HEAD_DIM, ML_HEAD_DIM + LANES), F32),
            pltpu.VMEM((nb * ML_HEADS, SUBLANES, LANES), F32),
        ],
        compiler_params=_cparams(("arbitrary", "arbitrary")),
        name="mlstm",
    )(proj3, proj3, proj3, proj3, ifc3, *([ifr] * nb), conv_w, conv_b, bias_col, bias_row, g_out)
    return y.reshape(bsz * seq, W)


def _memkv_kernel(mem_ref, g_ref, w_ref, gk_ref, k_ref, v_ref):
    x = mem_ref[...]
    ms = jnp.mean(x * x, axis=-1, keepdims=True)
    mn = (x * lax.rsqrt(ms + EPS) * g_ref[...]).astype(BF16)
    kv = _dot(mn, w_ref[...])
    W = XA_HEADS * XA_HEAD_DIM
    for h in range(XA_HEADS):
        sl = slice(h * XA_HEAD_DIM, (h + 1) * XA_HEAD_DIM)
        kh = kv[:, sl]
        kn = kh * lax.rsqrt(jnp.mean(kh * kh, axis=-1, keepdims=True) + EPS) * gk_ref[...]
        k_ref[:, sl] = kn.astype(BF16)
    v_ref[...] = kv[:, W:].astype(BF16)


def _memkv(mem2, g_mem, w_kv, g_k, bsz, n_mem):
    d = mem2.shape[1]
    W = XA_HEADS * XA_HEAD_DIM
    return pl.pallas_call(
        _memkv_kernel,
        grid=(bsz,),
        in_specs=[
            pl.BlockSpec((n_mem, d), lambda b: (b, 0)),
            pl.BlockSpec((1, d), lambda b: (0, 0)),
            pl.BlockSpec((d, 2 * W), lambda b: (0, 0)),
            pl.BlockSpec((1, XA_HEAD_DIM), lambda b: (0, 0)),
        ],
        out_specs=[pl.BlockSpec((n_mem, W), lambda b: (b, 0)),
                   pl.BlockSpec((n_mem, W), lambda b: (b, 0))],
        out_shape=[jax.ShapeDtypeStruct((bsz * n_mem, W), BF16)] * 2,
        compiler_params=_cparams(("arbitrary",)),
        name="memkv",
    )(mem2, g_mem, w_kv, g_k)


def _merge_kernel(x_ref, ysb_ref, yml_ref, xq_ref, kn_ref, vv_ref, gq_ref, gm_ref, wg_ref,
                  wsb_ref, wml_ref, wxa_ref, wo_ref, gf_ref, x1_ref, h2_ref):
    scale = 1.0 / math.sqrt(XA_HEAD_DIM)
    heads = []
    for hd in range(XA_HEADS):
        sl = slice(hd * XA_HEAD_DIM, (hd + 1) * XA_HEAD_DIM)
        qh = xq_ref[:, sl].astype(F32)
        qn = qh * lax.rsqrt(jnp.mean(qh * qh, axis=-1, keepdims=True) + EPS) * gq_ref[...]
        s = _dot_nt(qn.astype(BF16), kn_ref[:, sl]) * scale
        p = jnp.exp(s - jnp.max(s, axis=-1, keepdims=True))
        p = p / jnp.sum(p, axis=-1, keepdims=True)
        heads.append(_dot(p.astype(BF16), vv_ref[:, sl]).astype(BF16))
    yxa = jnp.concatenate(heads, axis=1)

    x = x_ref[...]
    d = x.shape[1]
    h = (x * lax.rsqrt(jnp.mean(x * x, axis=-1, keepdims=True) + EPS) * gm_ref[...]).astype(BF16)
    merged = None
    for br, (y, w_ref) in enumerate(((ysb_ref[...], wsb_ref), (yml_ref[...], wml_ref), (yxa, wxa_ref))):
        gate = _sigmoid(_dot_nt(h, wg_ref[br * d:(br + 1) * d, :]))
        term = gate * _dot(y, w_ref[...])
        merged = term if merged is None else merged + term
    x1 = x + _dot(merged.astype(BF16), wo_ref[...])
    x1_ref[...] = x1
    ms = jnp.mean(x1 * x1, axis=-1, keepdims=True)
    h2_ref[...] = (x1 * lax.rsqrt(ms + EPS) * gf_ref[...]).astype(BF16)


def _merge(x2, ysb, yml, proj, xq_col, kn, vv, g_q, g_mix, w_gate, w_sb, w_ml, w_xa, w_o, g_ffn, seq, tm):
    t, d = x2.shape
    wb = ysb.shape[1]
    n_mem = kn.shape[0] // (t // seq)
    tiles_per_seq = seq // tm

    def rows(width, cb=0):
        return pl.BlockSpec((tm, width), lambda i: (i, cb))

    def const(shape):
        return pl.BlockSpec(shape, lambda i: (0, 0), pipeline_mode=pl.Buffered(1))

    def per_batch(shape):
        return pl.BlockSpec(shape, lambda i: (i // tiles_per_seq, 0))

    return pl.pallas_call(
        _merge_kernel,
        grid=(t // tm,),
        in_specs=[
            rows(d), rows(wb), rows(wb), rows(wb, xq_col),
            per_batch((n_mem, wb)), per_batch((n_mem, wb)), const((1, XA_HEAD_DIM)),
            const((1, d)), const((3 * d, d)),
            const((wb, d)), const((wb, d)), const((wb, d)), const((d, d)), const((1, d)),
        ],
        out_specs=[rows(d), rows(d)],
        out_shape=[jax.ShapeDtypeStruct((t, d), F32), jax.ShapeDtypeStruct((t, d), BF16)],
        compiler_params=_cparams(("arbitrary",)),
        name="merge",
    )(x2, ysb, yml, proj, kn, vv, g_q, g_mix, w_gate, w_sb, w_ml, w_xa, w_o, g_ffn)


def _ffn_kernel(x1_ref, h2_ref, wup_ref, cw_ref, cb_ref, wd_ref, o_ref, tail_scr, act_scr,
                *, tm, tf, tiles_per_seq):
    i = pl.program_id(0)
    first = (i % tiles_per_seq) == 0
    th = tm // FFN_ROW_SPLIT

    for hh in range(FFN_ROW_SPLIT):
        rows = slice(hh * th, (hh + 1) * th)
        h2 = h2_ref[rows, :]
        for co in range(0, tf, LANES):
            cols = slice(2 * co, 2 * co + 2 * LANES)
            u = _dot(h2, wup_ref[:, cols])
            tail = tail_scr[:, cols]
            if hh == 0:
                tail = jnp.where(first, jnp.zeros_like(tail), tail)
            tail_scr[:, cols] = u[th - SUBLANES:th, :]
            head = jnp.concatenate([tail, u[0:SUBLANES, :]], axis=0)
            cw = cw_ref[:, cols]
            y = cb_ref[:, cols] + cw[FF_CONV - 1:FF_CONV, :] * u
            for back in range(1, FF_CONV):
                prev = jnp.concatenate(
                    [pltpu.roll(head, back, axis=0)[SUBLANES:, :], pltpu.roll(u, back, axis=0)[SUBLANES:, :]],
                    axis=0)
                y = y + cw[FF_CONV - 1 - back:FF_CONV - back, :] * prev
            uv, ug = y[:, :LANES], y[:, LANES:]
            act_scr[rows, co:co + LANES] = (ug * _sigmoid(ug) * uv).astype(BF16)
        o_ref[rows, :] = x1_ref[rows, :] + _dot(act_scr[rows, :], wd_ref[...])


def _interleave_val_gate(a):
    dff = a.shape[-1] // 2
    blocks = []
    for c in range(dff // LANES):
        blocks.append(a[..., c * LANES:(c + 1) * LANES])
        blocks.append(a[..., dff + c * LANES:dff + (c + 1) * LANES])
    return jnp.concatenate(blocks, axis=-1)


def _ffn(x1, h2, w_up, conv_w, conv_b, w_down, seq, tm):
    t, d = x1.shape
    dff = w_down.shape[0]

    def resident(shape):
        return pl.BlockSpec(shape, lambda i: (0, 0), pipeline_mode=pl.Buffered(1))

    return pl.pallas_call(
        functools.partial(_ffn_kernel, tm=tm, tf=dff, tiles_per_seq=seq // tm),
        grid=(t // tm,),
        in_specs=[
            pl.BlockSpec((tm, d), lambda i: (i, 0)),
            pl.BlockSpec((tm, d), lambda i: (i, 0)),
            resident((d, 2 * dff)),
            resident((FF_CONV, 2 * dff)),
            resident((1, 2 * dff)),
            resident((dff, d)),
        ],
        out_specs=pl.BlockSpec((tm, d), lambda i: (i, 0)),
        out_shape=jax.ShapeDtypeStruct((t, d), F32),
        scratch_shapes=[
            pltpu.VMEM((SUBLANES, 2 * dff), F32),
            pltpu.VMEM((tm, dff), BF16),
        ],
        compiler_params=_cparams(("arbitrary",)),
        name="ffn",
    )(x1, h2, _interleave_val_gate(w_up), _interleave_val_gate(conv_w), _interleave_val_gate(conv_b), w_down)


def kernel(x, mem, g_mix, w_in, ml_conv_w, ml_conv_b, ml_b_i, ml_b_f, ml_g_out, g_mem,
           w_mem_kv, xa_g_q, xa_g_k, w_sb_out, w_ml_out, w_xa_out, w_o, g_ffn, w_up,
           ff_conv_w, ff_conv_b, w_down):
    bsz, seq, d = x.shape
    n_mem = mem.shape[1]
    depth = w_in.shape[0]
    sbw = SB_HEADS * SB_HEAD_DIM
    mlw = ML_HEADS * ML_HEAD_DIM
    xaw = XA_HEADS * XA_HEAD_DIM
    n_wide = 3 * sbw + 4 * mlw
    if_lo, if_hi = n_wide, n_wide + 2 * ML_HEADS
    sb_scale = math.log2(math.e) / math.sqrt(SB_HEAD_DIM)

    x2 = x.reshape(bsz * seq, d)
    mem2 = mem.reshape(bsz * n_mem, d)
    for l in range(depth):
        wt = w_in[l].T
        w_main = jnp.concatenate(
            [wt[:sbw] * (-sb_scale), wt[sbw:n_wide], wt[if_hi:if_hi + xaw]], axis=0).astype(BF16)
        w_gate = wt[if_hi + xaw:].astype(BF16)
        w_ifc = jnp.pad(wt[if_lo:if_hi], ((0, LANES - 2 * ML_HEADS), (0, 0))).astype(BF16)
        proj, ifc, ifr = _inproj(x2, g_mix[l][None, :], w_main, w_ifc, tm=512, tn=4096)

        ysb = _sb_attention(proj, bsz, seq, qcol=0, kcol=sbw // LANES, vcol=2 * sbw // LANES, tq=512)

        bias = jnp.concatenate([ml_b_i[l], ml_b_f[l]])
        bias_col = jnp.pad(bias, (0, LANES - 2 * ML_HEADS))[None, :]
        bias_row = bias[:, None]
        ml0 = 3 * sbw // mlw
        yml = _mlstm(proj, ifc, ifr, ml_conv_w[l], ml_conv_b[l][None, :], bias_col, bias_row,
                     ml_g_out[l].reshape(1, mlw), bsz, seq, cols=(ml0, ml0 + 1, ml0 + 2, ml0 + 3), nb=ML_BATCH)

        kn, vv = _memkv(mem2, g_mem[l][None, :], w_mem_kv[l].astype(BF16), xa_g_k[l][None, :], bsz, n_mem)
        x1, h2 = _merge(x2, ysb, yml, proj, n_wide // xaw, kn, vv, xa_g_q[l][None, :],
                        g_mix[l][None, :], w_gate,
                        w_sb_out[l].astype(BF16), w_ml_out[l].astype(BF16), w_xa_out[l].astype(BF16),
                        w_o[l].astype(BF16), g_ffn[l][None, :], seq, tm=512)

        x2 = _ffn(x1, h2, w_up[l].astype(BF16), ff_conv_w[l], ff_conv_b[l][None, :],
                  w_down[l].astype(BF16), seq, tm=512)
    return x2.reshape(bsz, seq, d)
```

```python
import functools
import math

import jax
import jax.numpy as jnp
from jax import lax
from jax.experimental import pallas as pl
from jax.experimental.pallas import tpu as pltpu

F32 = jnp.float32
BF16 = jnp.bfloat16

SB_HEADS = 8
SB_HEAD_DIM = 64
ML_HEADS = 4
ML_HEAD_DIM = 128
ML_CONV = 4
ML_CHUNK = 128
XA_HEADS = 4
XA_HEAD_DIM = 128
FF_CONV = 3
EPS = 1e-6
SB_UNROLL = 2
SB_TILES_PER_STEP = 4
SB_DEAD_LOG2 = -150.0
ML_BATCH = 1
ML_CHUNKS_PER_STEP = 1
FFN_ROW_SPLIT = 2

LANES = 128
SUBLANES = 8
V7X_VMEM_BYTES = 64 * 1024 * 1024
VMEM_LIMIT = V7X_VMEM_BYTES * 7 // 8


def _cparams(sem):
    return pltpu.CompilerParams(dimension_semantics=sem, vmem_limit_bytes=VMEM_LIMIT)


def _dot(a, b):
    return jnp.dot(a, b, preferred_element_type=F32)


def _dot_nt(a, b):
    return lax.dot_general(a, b, (((1,), (1,)), ((), ())), preferred_element_type=F32)


def _dot_tn(a, b):
    return lax.dot_general(a, b, (((0,), (0,)), ((), ())), preferred_element_type=F32)


def _sigmoid(x):
    return 0.5 * jnp.tanh(0.5 * x) + 0.5


def _inproj_kernel(x_ref, g_ref, w_ref, wif_ref, o_ref, ifc_ref, ifr_ref, h_scr, *, tn):
    j = pl.program_id(1)

    @pl.when(j == 0)
    def _():
        x = x_ref[...]
        ms = jnp.mean(x * x, axis=-1, keepdims=True)
        h = (x * lax.rsqrt(ms + EPS) * g_ref[...]).astype(BF16)
        h_scr[...] = h
        ifc = _dot_nt(h, wif_ref[...])
        ifc_ref[...] = ifc
        ifr_ref[...] = ifc.T[0:SUBLANES, :]

    o_ref[...] = _dot_nt(h_scr[...], w_ref[pl.ds(pl.multiple_of(j * tn, LANES), tn), :]).astype(BF16)


def _inproj(x2, g, w_main, w_if, tm, tn):
    t, d = x2.shape
    n = w_main.shape[0]
    return pl.pallas_call(
        functools.partial(_inproj_kernel, tn=tn),
        grid=(t // tm, n // tn),
        in_specs=[
            pl.BlockSpec((tm, d), lambda i, j: (i, 0)),
            pl.BlockSpec((1, d), lambda i, j: (0, 0)),
            pl.BlockSpec((n, d), lambda i, j: (0, 0), pipeline_mode=pl.Buffered(1)),
            pl.BlockSpec((LANES, d), lambda i, j: (0, 0)),
        ],
        out_specs=[
            pl.BlockSpec((tm, tn), lambda i, j: (i, j)),
            pl.BlockSpec((tm, LANES), lambda i, j: (i, 0)),
            pl.BlockSpec((SUBLANES, tm), lambda i, j: (0, i)),
        ],
        out_shape=[
            jax.ShapeDtypeStruct((t, n), BF16),
            jax.ShapeDtypeStruct((t, LANES), F32),
            jax.ShapeDtypeStruct((SUBLANES, t), F32),
        ],
        scratch_shapes=[pltpu.VMEM((tm, d), BF16)],
        compiler_params=_cparams(("arbitrary", "arbitrary")),
        name="inproj",
    )(x2, g, w_main, w_if)


def _sb_kernel(q_ref, k_ref, v_ref, o_ref, kk_scr, vv_scr, acc_scr, carry_scr, w_scr, *, tq):
    i = pl.program_id(2)
    tk = LANES
    nsub = tq // tk
    seq = k_ref.shape[0]

    @pl.when(i == 0)
    def _():
        lane = lax.broadcasted_iota(jnp.int32, (seq, LANES), 1)
        for src, dst in ((k_ref, kk_scr), (v_ref, vv_scr)):
            a = src[...]
            zero = jnp.zeros_like(a)
            dst[:, 0:tk, :] = jnp.where(lane < SB_HEAD_DIM, a, zero).reshape(seq // tk, tk, LANES)
            dst[:, tk:2 * tk, :] = jnp.where(lane >= SB_HEAD_DIM, a, zero).reshape(seq // tk, tk, LANES)

    r2 = lax.broadcasted_iota(jnp.int32, (2 * tk, 2 * tk), 0)
    c2 = lax.broadcasted_iota(jnp.int32, (2 * tk, 2 * tk), 1)
    suffix = jnp.where(((r2 // tk) == (c2 // tk)) & ((r2 % tk) > (c2 % tk)), 1.0, 0.0).astype(BF16)

    def tile(sub):
        t = i * SB_TILES_PER_STEP + sub
        ro = sub * tq
        acc_scr[...] = jnp.zeros_like(acc_scr)
        carry_scr[...] = jnp.zeros_like(carry_scr)

        def step(kb_lo, nblk, tile_off):
            q = q_ref[ro:ro + tq, :]
            kk = kk_scr[pl.ds(kb_lo, nblk)].reshape(nblk * 2 * tk, LANES)
            y = _dot_nt(q, kk)
            for b in reversed(range(nblk)):
                sl = slice(b * 2 * tk, (b + 1) * 2 * tk)
                kb = -1 if tile_off is None else tile_off + b
                r0 = max(kb, 0) * tk
                n = tq - r0
                yb = y[r0:, sl]
                neg_abs = lax.bitcast_convert_type(
                    lax.bitcast_convert_type(yb, jnp.uint32) | jnp.uint32(0x80000000), F32)
                lf = jnp.minimum(yb, 0.0) - jnp.log2(1.0 + jnp.exp2(neg_abs))
                if kb >= 0:
                    strict = (lax.broadcasted_iota(jnp.int32, (n, 2 * tk), 1) % tk
                              < lax.broadcasted_iota(jnp.int32, (n, 2 * tk), 0))
                    lf = jnp.where(strict, lf, 0.0)
                suf = _dot(lf.astype(BF16), suffix)
                carry = carry_scr[r0:tq, :]
                w = jnp.exp2((lf - yb) + (suf + carry))
                if kb >= 0:
                    w = jnp.where(strict, w, 0.0)
                w_scr[r0:tq, sl] = w.astype(BF16)
                if r0 > 0:
                    w_scr[0:r0, sl] = jnp.zeros((r0, 2 * tk), BF16)
                tot0 = jnp.sum(lf[:, :tk], axis=1, keepdims=True)
                tot1 = jnp.sum(lf[:, tk:], axis=1, keepdims=True)
                carry_scr[r0:tq, :] = carry + jnp.concatenate(
                    [jnp.broadcast_to(tot0, (n, tk)), jnp.broadcast_to(tot1, (n, tk))], axis=1)
            vv = vv_scr[pl.ds(kb_lo, nblk)].reshape(nblk * 2 * tk, LANES)
            acc_scr[...] += _dot(w_scr[:, 0:nblk * 2 * tk], vv)

        if sub == 0:
            @pl.when(t == 0)
            def _():
                step(0, nsub, 0)

            @pl.when(t > 0)
            def _():
                step(t * nsub - SB_UNROLL, nsub + SB_UNROLL, -SB_UNROLL)
        else:
            step(t * nsub - SB_UNROLL, nsub + SB_UNROLL, -SB_UNROLL)

        n_steps = jnp.maximum(t * nsub // SB_UNROLL - 1, 0)

        def cond(state):
            it, live = state
            return jnp.logical_and(it < n_steps, live)

        def body(state):
            it, _ = state
            step(t * nsub - SB_UNROLL * (it + 2), SB_UNROLL, None)
            return it + 1, jnp.max(carry_scr[...]) >= SB_DEAD_LOG2

        lax.while_loop(cond, body, (jnp.int32(0), jnp.max(carry_scr[...]) >= SB_DEAD_LOG2))
        o_ref[ro:ro + tq, :] = acc_scr[...].astype(BF16)

    for sub in range(SB_TILES_PER_STEP):
        tile(sub)


def _sb_attention(proj, bsz, seq, qcol, kcol, vcol, tq):
    npair = SB_HEADS * SB_HEAD_DIM // LANES
    rows = tq * SB_TILES_PER_STEP
    nq = seq // rows
    return pl.pallas_call(
        functools.partial(_sb_kernel, tq=tq),
        grid=(bsz, npair, nq),
        in_specs=[
            pl.BlockSpec((rows, LANES), lambda b, p, i: (b * nq + i, qcol + p)),
            pl.BlockSpec((seq, LANES), lambda b, p, i: (b, kcol + p)),
            pl.BlockSpec((seq, LANES), lambda b, p, i: (b, vcol + p)),
        ],
        out_specs=pl.BlockSpec((rows, LANES), lambda b, p, i: (b * nq + i, p)),
        out_shape=jax.ShapeDtypeStruct((bsz * seq, npair * LANES), BF16),
        scratch_shapes=[
            pltpu.VMEM((seq // LANES, 2 * LANES, LANES), BF16),
            pltpu.VMEM((seq // LANES, 2 * LANES, LANES), BF16),
            pltpu.VMEM((tq, LANES), F32),
            pltpu.VMEM((tq, 2 * LANES), F32),
            pltpu.VMEM((tq, (tq // LANES + SB_UNROLL) * 2 * LANES), BF16),
        ],
        compiler_params=_cparams(("arbitrary", "arbitrary", "arbitrary")),
        name="sb_attn",
    )(proj, proj, proj)


def _mlstm_kernel(*refs, nb):
    q_ref, k_ref, v_ref, o_ref, ifc_ref = refs[:5]
    ifr_refs = refs[5:5 + nb]
    cw_ref, cb_ref, bic_ref, bir_ref, gout_ref, y_ref, qtail, ktail, ct_scr, m_scr = refs[5 + nb:]
    c = pl.program_id(1)
    L = ML_CHUNK
    W = ML_HEADS * ML_HEAD_DIM

    @pl.when(c == 0)
    def _():
        qtail[...] = jnp.zeros_like(qtail)
        ktail[...] = jnp.zeros_like(ktail)
        ct_scr[...] = jnp.zeros_like(ct_scr)
        m_scr[...] = jnp.zeros_like(m_scr)

    stores = []
    m_all = m_scr[...]
    m_cur = [m_all[st, 0:1, 0:1] for st in range(nb * ML_HEADS)]
    ct_cur = [ct_scr[st] for st in range(nb * ML_HEADS)]
    qtails = [qtail[r] for r in range(nb)]
    ktails = [ktail[r] for r in range(nb)]

    def conv_silu(x, tails, r, w, b):
        head = jnp.concatenate([tails[r], x[0:SUBLANES, :]], axis=0)
        tails[r] = x[L - SUBLANES:L, :]
        y = b + w[ML_CONV - 1:ML_CONV, :] * x
        for back in range(1, ML_CONV):
            prev = jnp.concatenate(
                [pltpu.roll(head, back, axis=0)[SUBLANES:, :], pltpu.roll(x, back, axis=0)[SUBLANES:, :]],
                axis=0)
            y = y + w[ML_CONV - 1 - back:ML_CONV - back, :] * prev
        return y * _sigmoid(y)

    cw = cw_ref[...]
    cb = cb_ref[...]
    row = lax.broadcasted_iota(jnp.int32, (L, L), 0)
    col = lax.broadcasted_iota(jnp.int32, (L, L), 1)
    causal = col <= row
    ones = jnp.ones((L, LANES), BF16)

    for ch, r in [(ch, r) for ch in range(ML_CHUNKS_PER_STEP) for r in range(nb)]:
        rows = slice(ch * L, (ch + 1) * L)
        qc = conv_silu(q_ref[r, rows, :].astype(F32), qtails, r, cw[:, :W], cb[:, :W])
        kc = conv_silu(k_ref[r, rows, :].astype(F32), ktails, r, cw[:, W:], cb[:, W:])
        kc = kc * (1.0 / math.sqrt(ML_HEAD_DIM))
        ifc = ifc_ref[r, rows, :] + bic_ref[...]
        ifr = ifr_refs[r][:, rows] + bir_ref[...]

        for h in range(ML_HEADS):
            st = r * ML_HEADS + h
            sl = slice(h * ML_HEAD_DIM, (h + 1) * ML_HEAD_DIM)
            i_col = ifc[:, h:h + 1]
            f_col = ifc[:, ML_HEADS + h:ML_HEADS + h + 1]
            i_row = ifr[h:h + 1, :]
            f_row = ifr[ML_HEADS + h:ML_HEADS + h + 1, :]
            lf_col = jnp.minimum(f_col, 0.0) - jnp.log(1.0 + jnp.exp(-jnp.abs(f_col)))
            lf_row = jnp.minimum(f_row, 0.0) - jnp.log(1.0 + jnp.exp(-jnp.abs(f_row)))
            b_col = jnp.sum(jnp.where(causal, lf_row, 0.0), axis=1, keepdims=True)
            b_row = jnp.sum(jnp.where(row <= col, lf_col, 0.0), axis=0, keepdims=True)
            b_last = jnp.sum(lf_row, axis=1, keepdims=True)
            m_st = m_cur[st]
            log_d = jnp.where(causal, b_col - b_row + i_row, -jnp.inf)
            m_inter = b_col + m_st
            m_t = jnp.maximum(m_inter, jnp.max(log_d, axis=1, keepdims=True))
            q_f = qc[:, sl]
            k_h = kc[:, sl]
            v_ext = jnp.concatenate([v_ref[r, rows, sl], ones], axis=1)
            s = _dot_nt(q_f.astype(BF16), k_h.astype(BF16)) * jnp.exp(log_d - m_t)
            w_inter = jnp.exp(m_inter - m_t)
            ct = ct_cur[st]
            num = _dot(jnp.concatenate([s.astype(BF16), (w_inter * q_f).astype(BF16)], axis=1),
                       jnp.concatenate([v_ext, ct.astype(BF16)], axis=0))
            den = num[:, ML_HEAD_DIM:]
            hh = num[:, :ML_HEAD_DIM] / jnp.maximum(jnp.abs(den), jnp.exp(-m_t))

            log_g = b_last - b_col + i_col
            m_new = jnp.maximum(b_last + m_st, jnp.max(log_g, axis=0, keepdims=True))
            decay = jnp.exp(b_last + m_st - m_new)
            wk = jnp.exp(log_g - m_new)
            ct_cur[st] = decay * ct + _dot_tn((wk * k_h).astype(BF16), v_ext)
            m_cur[st] = m_new

            hn = hh * lax.rsqrt(jnp.mean(hh * hh, axis=-1, keepdims=True) + EPS) * gout_ref[:, sl]
            stores.append((y_ref, (r, rows, sl), (_sigmoid(o_ref[r, rows, sl].astype(F32)) * hn).astype(BF16)))

    for st in range(nb * ML_HEADS):
        stores.append((ct_scr, st, ct_cur[st]))
        stores.append((m_scr, st, jnp.broadcast_to(m_cur[st], (SUBLANES, LANES))))
    for r in range(nb):
        stores.append((qtail, r, qtails[r]))
        stores.append((ktail, r, ktails[r]))
    for ref, idx, val in stores:
        ref[idx] = val


def _mlstm(proj, ifc, ifr, conv_w, conv_b, bias_col, bias_row, g_out, bsz, seq, cols, nb):
    qcol, kcol, vcol, ocol = cols
    L = ML_CHUNK * ML_CHUNKS_PER_STEP
    W = ML_HEADS * ML_HEAD_DIM
    nc = seq // L
    proj3 = proj.reshape(bsz, seq, proj.shape[1])
    ifc3 = ifc.reshape(bsz, seq, LANES)

    def colspec(cb):
        return pl.BlockSpec((nb, L, W), lambda b, c: (b, c, cb))

    def const(shape):
        return pl.BlockSpec(shape, lambda b, c: (0,) * len(shape))

    ifr_specs = [pl.BlockSpec((SUBLANES, L), lambda b, c, r=r: (0, (b * nb + r) * nc + c)) for r in range(nb)]
    y = pl.pallas_call(
        functools.partial(_mlstm_kernel, nb=nb),
        grid=(bsz // nb, nc),
        in_specs=[
            colspec(qcol), colspec(kcol), colspec(vcol), colspec(ocol),
            pl.BlockSpec((nb, L, LANES), lambda b, c: (b, c, 0)),
            *ifr_specs,
            const((ML_CONV, 2 * W)), const((1, 2 * W)),
            const((1, LANES)), const((SUBLANES, 1)), const((1, W)),
        ],
        out_specs=pl.BlockSpec((nb, L, W), lambda b, c: (b, c, 0)),
        out_shape=jax.ShapeDtypeStruct((bsz, seq, W), BF16),
        scratch_shapes=[
            pltpu.VMEM((nb, SUBLANES, W), F32),
            pltpu.VMEM((nb, SUBLANES, W), F32),
            pltpu.VMEM((nb * ML_HEADS, ML_HEAD_DIM, ML_HEAD_DIM + LANES), F32),
            pltpu.VMEM((nb * ML_HEADS, SUBLANES, LANES), F32),
        ],
        compiler_params=_cparams(("arbitrary", "arbitrary")),
        name="mlstm",
    )(proj3, proj3, proj3, proj3, ifc3, *([ifr] * nb), conv_w, conv_b, bias_col, bias_row, g_out)
    return y.reshape(bsz * seq, W)


def _memkv_kernel(mem_ref, g_ref, w_ref, gk_ref, k_ref, v_ref):
    x = mem_ref[...]
    ms = jnp.mean(x * x, axis=-1, keepdims=True)
    mn = (x * lax.rsqrt(ms + EPS) * g_ref[...]).astype(BF16)
    kv = _dot(mn, w_ref[...])
    W = XA_HEADS * XA_HEAD_DIM
    for h in range(XA_HEADS):
        sl = slice(h * XA_HEAD_DIM, (h + 1) * XA_HEAD_DIM)
        kh = kv[:, sl]
        kn = kh * lax.rsqrt(jnp.mean(kh * kh, axis=-1, keepdims=True) + EPS) * gk_ref[...]
        k_ref[:, sl] = kn.astype(BF16)
    v_ref[...] = kv[:, W:].astype(BF16)


def _memkv(mem2, g_mem, w_kv, g_k, bsz, n_mem):
    d = mem2.shape[1]
    W = XA_HEADS * XA_HEAD_DIM
    return pl.pallas_call(
        _memkv_kernel,
        grid=(bsz,),
        in_specs=[
            pl.BlockSpec((n_mem, d), lambda b: (b, 0)),
            pl.BlockSpec((1, d), lambda b: (0, 0)),
            pl.BlockSpec((d, 2 * W), lambda b: (0, 0)),
            pl.BlockSpec((1, XA_HEAD_DIM), lambda b: (0, 0)),
        ],
        out_specs=[pl.BlockSpec((n_mem, W), lambda b: (b, 0)),
                   pl.BlockSpec((n_mem, W), lambda b: (b, 0))],
        out_shape=[jax.ShapeDtypeStruct((bsz * n_mem, W), BF16)] * 2,
        compiler_params=_cparams(("arbitrary",)),
        name="memkv",
    )(mem2, g_mem, w_kv, g_k)


def _merge_kernel(x_ref, ysb_ref, yml_ref, xq_ref, kn_ref, vv_ref, gq_ref, gm_ref, wg_ref,
                  wsb_ref, wml_ref, wxa_ref, wo_ref, gf_ref, x1_ref, h2_ref):
    scale = 1.0 / math.sqrt(XA_HEAD_DIM)
    heads = []
    for hd in range(XA_HEADS):
        sl = slice(hd * XA_HEAD_DIM, (hd + 1) * XA_HEAD_DIM)
        qh = xq_ref[:, sl].astype(F32)
        qn = qh * lax.rsqrt(jnp.mean(qh * qh, axis=-1, keepdims=True) + EPS) * gq_ref[...]
        s = _dot_nt(qn.astype(BF16), kn_ref[:, sl]) * scale
        p = jnp.exp(s - jnp.max(s, axis=-1, keepdims=True))
        p = p / jnp.sum(p, axis=-1, keepdims=True)
        heads.append(_dot(p.astype(BF16), vv_ref[:, sl]).astype(BF16))
    yxa = jnp.concatenate(heads, axis=1)

    x = x_ref[...]
    d = x.shape[1]
    h = (x * lax.rsqrt(jnp.mean(x * x, axis=-1, keepdims=True) + EPS) * gm_ref[...]).astype(BF16)
    merged = None
    for br, (y, w_ref) in enumerate(((ysb_ref[...], wsb_ref), (yml_ref[...], wml_ref), (yxa, wxa_ref))):
        gate = _sigmoid(_dot_nt(h, wg_ref[br * d:(br + 1) * d, :]))
        term = gate * _dot(y, w_ref[...])
        merged = term if merged is None else merged + term
    x1 = x + _dot(merged.astype(BF16), wo_ref[...])
    x1_ref[...] = x1
    ms = jnp.mean(x1 * x1, axis=-1, keepdims=True)
    h2_ref[...] = (x1 * lax.rsqrt(ms + EPS) * gf_ref[...]).astype(BF16)


def _merge(x2, ysb, yml, proj, xq_col, kn, vv, g_q, g_mix, w_gate, w_sb, w_ml, w_xa, w_o, g_ffn, seq, tm):
    t, d = x2.shape
    wb = ysb.shape[1]
    n_mem = kn.shape[0] // (t // seq)
    tiles_per_seq = seq // tm

    def rows(width, cb=0):
        return pl.BlockSpec((tm, width), lambda i: (i, cb))

    def const(shape):
        return pl.BlockSpec(shape, lambda i: (0, 0), pipeline_mode=pl.Buffered(1))

    def per_batch(shape):
        return pl.BlockSpec(shape, lambda i: (i // tiles_per_seq, 0))

    return pl.pallas_call(
        _merge_kernel,
        grid=(t // tm,),
        in_specs=[
            rows(d), rows(wb), rows(wb), rows(wb, xq_col),
            per_batch((n_mem, wb)), per_batch((n_mem, wb)), const((1, XA_HEAD_DIM)),
            const((1, d)), const((3 * d, d)),
            const((wb, d)), const((wb, d)), const((wb, d)), const((d, d)), const((1, d)),
        ],
        out_specs=[rows(d), rows(d)],
        out_shape=[jax.ShapeDtypeStruct((t, d), F32), jax.ShapeDtypeStruct((t, d), BF16)],
        compiler_params=_cparams(("arbitrary",)),
        name="merge",
    )(x2, ysb, yml, proj, kn, vv, g_q, g_mix, w_gate, w_sb, w_ml, w_xa, w_o, g_ffn)


def _ffn_kernel(x1_ref, h2_ref, wup_ref, cw_ref, cb_ref, wd_ref, o_ref, tail_scr, act_scr,
                *, tm, tf, tiles_per_seq):
    i = pl.program_id(0)
    first = (i % tiles_per_seq) == 0
    th = tm // FFN_ROW_SPLIT

    for hh in range(FFN_ROW_SPLIT):
        rows = slice(hh * th, (hh + 1) * th)
        h2 = h2_ref[rows, :]
        for co in range(0, tf, LANES):
            cols = slice(2 * co, 2 * co + 2 * LANES)
            u = _dot(h2, wup_ref[:, cols])
            tail = tail_scr[:, cols]
            if hh == 0:
                tail = jnp.where(first, jnp.zeros_like(tail), tail)
            tail_scr[:, cols] = u[th - SUBLANES:th, :]
            head = jnp.concatenate([tail, u[0:SUBLANES, :]], axis=0)
            cw = cw_ref[:, cols]
            y = cb_ref[:, cols] + cw[FF_CONV - 1:FF_CONV, :] * u
            for back in range(1, FF_CONV):
                prev = jnp.concatenate(
                    [pltpu.roll(head, back, axis=0)[SUBLANES:, :], pltpu.roll(u, back, axis=0)[SUBLANES:, :]],
                    axis=0)
                y = y + cw[FF_CONV - 1 - back:FF_CONV - back, :] * prev
            uv, ug = y[:, :LANES], y[:, LANES:]
            act_scr[rows, co:co + LANES] = (ug * _sigmoid(ug) * uv).astype(BF16)
        o_ref[rows, :] = x1_ref[rows, :] + _dot(act_scr[rows, :], wd_ref[...])


def _interleave_val_gate(a):
    dff = a.shape[-1] // 2
    blocks = []
    for c in range(dff // LANES):
        blocks.append(a[..., c * LANES:(c + 1) * LANES])
        blocks.append(a[..., dff + c * LANES:dff + (c + 1) * LANES])
    return jnp.concatenate(blocks, axis=-1)


def _ffn(x1, h2, w_up, conv_w, conv_b, w_down, seq, tm):
    t, d = x1.shape
    dff = w_down.shape[0]

    def resident(shape):
        return pl.BlockSpec(shape, lambda i: (0, 0), pipeline_mode=pl.Buffered(1))

    return pl.pallas_call(
        functools.partial(_ffn_kernel, tm=tm, tf=dff, tiles_per_seq=seq // tm),
        grid=(t // tm,),
        in_specs=[
            pl.BlockSpec((tm, d), lambda i: (i, 0)),
            pl.BlockSpec((tm, d), lambda i: (i, 0)),
            resident((d, 2 * dff)),
            resident((FF_CONV, 2 * dff)),
            resident((1, 2 * dff)),
            resident((dff, d)),
        ],
        out_specs=pl.BlockSpec((tm, d), lambda i: (i, 0)),
        out_shape=jax.ShapeDtypeStruct((t, d), F32),
        scratch_shapes=[
            pltpu.VMEM((SUBLANES, 2 * dff), F32),
            pltpu.VMEM((tm, dff), BF16),
        ],
        compiler_params=_cparams(("arbitrary",)),
        name="ffn",
    )(x1, h2, _interleave_val_gate(w_up), _interleave_val_gate(conv_w), _interleave_val_gate(conv_b), w_down)


def kernel(x, mem, g_mix, w_in, ml_conv_w, ml_conv_b, ml_b_i, ml_b_f, ml_g_out, g_mem,
           w_mem_kv, xa_g_q, xa_g_k, w_sb_out, w_ml_out, w_xa_out, w_o, g_ffn, w_up,
           ff_conv_w, ff_conv_b, w_down):
    bsz, seq, d = x.shape
    n_mem = mem.shape[1]
    depth = w_in.shape[0]
    sbw = SB_HEADS * SB_HEAD_DIM
    mlw = ML_HEADS * ML_HEAD_DIM
    xaw = XA_HEADS * XA_HEAD_DIM
    n_wide = 3 * sbw + 4 * mlw
    if_lo, if_hi = n_wide, n_wide + 2 * ML_HEADS
    sb_scale = math.log2(math.e) / math.sqrt(SB_HEAD_DIM)

    x2 = x.reshape(bsz * seq, d)
    mem2 = mem.reshape(bsz * n_mem, d)
    for l in range(depth):
        wt = w_in[l].T
        w_main = jnp.concatenate(
            [wt[:sbw] * (-sb_scale), wt[sbw:n_wide], wt[if_hi:if_hi + xaw]], axis=0).astype(BF16)
        w_gate = wt[if_hi + xaw:].astype(BF16)
        w_ifc = jnp.pad(wt[if_lo:if_hi], ((0, LANES - 2 * ML_HEADS), (0, 0))).astype(BF16)
        proj, ifc, ifr = _inproj(x2, g_mix[l][None, :], w_main, w_ifc, tm=512, tn=4096)

        ysb = _sb_attention(proj, bsz, seq, qcol=0, kcol=sbw // LANES, vcol=2 * sbw // LANES, tq=512)

        bias = jnp.concatenate([ml_b_i[l], ml_b_f[l]])
        bias_col = jnp.pad(bias, (0, LANES - 2 * ML_HEADS))[None, :]
        bias_row = bias[:, None]
        ml0 = 3 * sbw // mlw
        yml = _mlstm(proj, ifc, ifr, ml_conv_w[l], ml_conv_b[l][None, :], bias_col, bias_row,
                     ml_g_out[l].reshape(1, mlw), bsz, seq, cols=(ml0, ml0 + 1, ml0 + 2, ml0 + 3), nb=ML_BATCH)

        kn, vv = _memkv(mem2, g_mem[l][None, :], w_mem_kv[l].astype(BF16), xa_g_k[l][None, :], bsz, n_mem)
        x1, h2 = _merge(x2, ysb, yml, proj, n_wide // xaw, kn, vv, xa_g_q[l][None, :],
                        g_mix[l][None, :], w_gate,
                        w_sb_out[l].astype(BF16), w_ml_out[l].astype(BF16), w_xa_out[l].astype(BF16),
                        w_o[l].astype(BF16), g_ffn[l][None, :], seq, tm=512)

        x2 = _ffn(x1, h2, w_up[l].astype(BF16), ff_conv_w[l], ff_conv_b[l][None, :],
                  w_down[l].astype(BF16), seq, tm=512)
    return x2.reshape(bsz, seq, d)
```

```python
import functools
import math

import jax
import jax.numpy as jnp
from jax import lax
from jax.experimental import pallas as pl
from jax.experimental.pallas import tpu as pltpu

F32 = jnp.float32
BF16 = jnp.bfloat16

SB_HEADS = 8
SB_HEAD_DIM = 64
ML_HEADS = 4
ML_HEAD_DIM = 128
ML_CONV = 4
ML_CHUNK = 128
XA_HEADS = 4
XA_HEAD_DIM = 128
FF_CONV = 3
EPS = 1e-6
SB_UNROLL = 2
SB_TILES_PER_STEP = 8
SB_DEAD_LOG2 = -150.0
ML_BATCH = 1
ML_CHUNKS_PER_STEP = 1
FFN_ROW_SPLIT = 2

LANES = 128
SUBLANES = 8
V7X_VMEM_BYTES = 64 * 1024 * 1024
VMEM_LIMIT = V7X_VMEM_BYTES * 7 // 8


def _cparams(sem):
    return pltpu.CompilerParams(dimension_semantics=sem, vmem_limit_bytes=VMEM_LIMIT)


def _dot(a, b):
    return jnp.dot(a, b, preferred_element_type=F32)


def _dot_nt(a, b):
    return lax.dot_general(a, b, (((1,), (1,)), ((), ())), preferred_element_type=F32)


def _dot_tn(a, b):
    return lax.dot_general(a, b, (((0,), (0,)), ((), ())), preferred_element_type=F32)


def _sigmoid(x):
    return 0.5 * jnp.tanh(0.5 * x) + 0.5


def _inproj_kernel(x_ref, g_ref, w_ref, wif_ref, o_ref, ifc_ref, ifr_ref, h_scr, *, tn):
    j = pl.program_id(1)

    @pl.when(j == 0)
    def _():
        x = x_ref[...]
        ms = jnp.mean(x * x, axis=-1, keepdims=True)
        h = (x * lax.rsqrt(ms + EPS) * g_ref[...]).astype(BF16)
        h_scr[...] = h
        ifc = _dot_nt(h, wif_ref[...])
        ifc_ref[...] = ifc
        ifr_ref[...] = ifc.T[0:SUBLANES, :]

    o_ref[...] = _dot_nt(h_scr[...], w_ref[pl.ds(pl.multiple_of(j * tn, LANES), tn), :]).astype(BF16)


def _inproj(x2, g, w_main, w_if, tm, tn):
    t, d = x2.shape
    n = w_main.shape[0]
    return pl.pallas_call(
        functools.partial(_inproj_kernel, tn=tn),
        grid=(t // tm, n // tn),
        in_specs=[
            pl.BlockSpec((tm, d), lambda i, j: (i, 0)),
            pl.BlockSpec((1, d), lambda i, j: (0, 0)),
            pl.BlockSpec((n, d), lambda i, j: (0, 0), pipeline_mode=pl.Buffered(1)),
            pl.BlockSpec((LANES, d), lambda i, j: (0, 0)),
        ],
        out_specs=[
            pl.BlockSpec((tm, tn), lambda i, j: (i, j)),
            pl.BlockSpec((tm, LANES), lambda i, j: (i, 0)),
            pl.BlockSpec((SUBLANES, tm), lambda i, j: (0, i)),
        ],
        out_shape=[
            jax.ShapeDtypeStruct((t, n), BF16),
            jax.ShapeDtypeStruct((t, LANES), F32),
            jax.ShapeDtypeStruct((SUBLANES, t), F32),
        ],
        scratch_shapes=[pltpu.VMEM((tm, d), BF16)],
        compiler_params=_cparams(("arbitrary", "arbitrary")),
        name="inproj",
    )(x2, g, w_main, w_if)


def _sb_kernel(q_ref, k_ref, v_ref, o_ref, kk_scr, vv_scr, acc_scr, carry_scr, w_scr, *, tq):
    i = pl.program_id(2)
    tk = LANES
    nsub = tq // tk
    seq = k_ref.shape[0]

    @pl.when(i == 0)
    def _():
        lane = lax.broadcasted_iota(jnp.int32, (seq, LANES), 1)
        for src, dst in ((k_ref, kk_scr), (v_ref, vv_scr)):
            a = src[...]
            zero = jnp.zeros_like(a)
            dst[:, 0:tk, :] = jnp.where(lane < SB_HEAD_DIM, a, zero).reshape(seq // tk, tk, LANES)
            dst[:, tk:2 * tk, :] = jnp.where(lane >= SB_HEAD_DIM, a, zero).reshape(seq // tk, tk, LANES)

    r2 = lax.broadcasted_iota(jnp.int32, (2 * tk, 2 * tk), 0)
    c2 = lax.broadcasted_iota(jnp.int32, (2 * tk, 2 * tk), 1)
    suffix = jnp.where(((r2 // tk) == (c2 // tk)) & ((r2 % tk) > (c2 % tk)), 1.0, 0.0).astype(BF16)

    def tile(sub):
        t = i * SB_TILES_PER_STEP + sub
        ro = sub * tq
        acc_scr[...] = jnp.zeros_like(acc_scr)
        carry_scr[...] = jnp.zeros_like(carry_scr)

        def step(kb_lo, nblk, tile_off):
            q = q_ref[ro:ro + tq, :]
            kk = kk_scr[pl.ds(kb_lo, nblk)].reshape(nblk * 2 * tk, LANES)
            y = _dot_nt(q, kk)
            for b in reversed(range(nblk)):
                sl = slice(b * 2 * tk, (b + 1) * 2 * tk)
                kb = -1 if tile_off is None else tile_off + b
                r0 = max(kb, 0) * tk
                n = tq - r0
                yb = y[r0:, sl]
                neg_abs = lax.bitcast_convert_type(
                    lax.bitcast_convert_type(yb, jnp.uint32) | jnp.uint32(0x80000000), F32)
                lf = jnp.minimum(yb, 0.0) - jnp.log2(1.0 + jnp.exp2(neg_abs))
                if kb >= 0:
                    strict = (lax.broadcasted_iota(jnp.int32, (n, 2 * tk), 1) % tk
                              < lax.broadcasted_iota(jnp.int32, (n, 2 * tk), 0))
                    lf = jnp.where(strict, lf, 0.0)
                suf = _dot(lf.astype(BF16), suffix)
                carry = carry_scr[r0:tq, :]
                w = jnp.exp2((lf - yb) + (suf + carry))
                if kb >= 0:
                    w = jnp.where(strict, w, 0.0)
                w_scr[r0:tq, sl] = w.astype(BF16)
                if r0 > 0:
                    w_scr[0:r0, sl] = jnp.zeros((r0, 2 * tk), BF16)
                tot0 = jnp.sum(lf[:, :tk], axis=1, keepdims=True)
                tot1 = jnp.sum(lf[:, tk:], axis=1, keepdims=True)
                carry_scr[r0:tq, :] = carry + jnp.concatenate(
                    [jnp.broadcast_to(tot0, (n, tk)), jnp.broadcast_to(tot1, (n, tk))], axis=1)
            vv = vv_scr[pl.ds(kb_lo, nblk)].reshape(nblk * 2 * tk, LANES)
            acc_scr[...] += _dot(w_scr[:, 0:nblk * 2 * tk], vv)

        if sub == 0:
            @pl.when(t == 0)
            def _():
                step(0, nsub, 0)

            @pl.when(t > 0)
            def _():
                step(t * nsub - SB_UNROLL, nsub + SB_UNROLL, -SB_UNROLL)
        else:
            step(t * nsub - SB_UNROLL, nsub + SB_UNROLL, -SB_UNROLL)

        n_steps = jnp.maximum(t * nsub // SB_UNROLL - 1, 0)

        def cond(state):
            it, live = state
            return jnp.logical_and(it < n_steps, live)

        def body(state):
            it, _ = state
            step(t * nsub - SB_UNROLL * (it + 2), SB_UNROLL, None)
            return it + 1, jnp.max(carry_scr[...]) >= SB_DEAD_LOG2

        lax.while_loop(cond, body, (jnp.int32(0), jnp.max(carry_scr[...]) >= SB_DEAD_LOG2))
        o_ref[ro:ro + tq, :] = acc_scr[...].astype(BF16)

    for sub in range(SB_TILES_PER_STEP):
        tile(sub)


def _sb_attention(proj, bsz, seq, qcol, kcol, vcol, tq):
    npair = SB_HEADS * SB_HEAD_DIM // LANES
    rows = tq * SB_TILES_PER_STEP
    nq = seq // rows
    return pl.pallas_call(
        functools.partial(_sb_kernel, tq=tq),
        grid=(bsz, npair, nq),
        in_specs=[
            pl.BlockSpec((rows, LANES), lambda b, p, i: (b * nq + i, qcol + p)),
            pl.BlockSpec((seq, LANES), lambda b, p, i: (b, kcol + p)),
            pl.BlockSpec((seq, LANES), lambda b, p, i: (b, vcol + p)),
        ],
        out_specs=pl.BlockSpec((rows, LANES), lambda b, p, i: (b * nq + i, p)),
        out_shape=jax.ShapeDtypeStruct((bsz * seq, npair * LANES), BF16),
        scratch_shapes=[
            pltpu.VMEM((seq // LANES, 2 * LANES, LANES), BF16),
            pltpu.VMEM((seq // LANES, 2 * LANES, LANES), BF16),
            pltpu.VMEM((tq, LANES), F32),
            pltpu.VMEM((tq, 2 * LANES), F32),
            pltpu.VMEM((tq, (tq // LANES + SB_UNROLL) * 2 * LANES), BF16),
        ],
        compiler_params=_cparams(("arbitrary", "arbitrary", "arbitrary")),
        name="sb_attn",
    )(proj, proj, proj)


def _mlstm_kernel(*refs, nb):
    q_ref, k_ref, v_ref, o_ref, ifc_ref = refs[:5]
    ifr_refs = refs[5:5 + nb]
    cw_ref, cb_ref, bic_ref, bir_ref, gout_ref, y_ref, qtail, ktail, ct_scr, m_scr = refs[5 + nb:]
    c = pl.program_id(1)
    L = ML_CHUNK
    W = ML_HEADS * ML_HEAD_DIM

    @pl.when(c == 0)
    def _():
        qtail[...] = jnp.zeros_like(qtail)
        ktail[...] = jnp.zeros_like(ktail)
        ct_scr[...] = jnp.zeros_like(ct_scr)
        m_scr[...] = jnp.zeros_like(m_scr)

    stores = []
    m_all = m_scr[...]
    m_cur = [m_all[st, 0:1, 0:1] for st in range(nb * ML_HEADS)]
    ct_cur = [ct_scr[st] for st in range(nb * ML_HEADS)]
    qtails = [qtail[r] for r in range(nb)]
    ktails = [ktail[r] for r in range(nb)]

    def conv_silu(x, tails, r, w, b):
        head = jnp.concatenate([tails[r], x[0:SUBLANES, :]], axis=0)
        tails[r] = x[L - SUBLANES:L, :]
        y = b + w[ML_CONV - 1:ML_CONV, :] * x
        for back in range(1, ML_CONV):
            prev = jnp.concatenate(
                [pltpu.roll(head, back, axis=0)[SUBLANES:, :], pltpu.roll(x, back, axis=0)[SUBLANES:, :]],
                axis=0)
            y = y + w[ML_CONV - 1 - back:ML_CONV - back, :] * prev
        return y * _sigmoid(y)

    cw = cw_ref[...]
    cb = cb_ref[...]
    row = lax.broadcasted_iota(jnp.int32, (L, L), 0)
    col = lax.broadcasted_iota(jnp.int32, (L, L), 1)
    causal = col <= row
    ones = jnp.ones((L, LANES), BF16)

    for ch, r in [(ch, r) for ch in range(ML_CHUNKS_PER_STEP) for r in range(nb)]:
        rows = slice(ch * L, (ch + 1) * L)
        qc = conv_silu(q_ref[r, rows, :].astype(F32), qtails, r, cw[:, :W], cb[:, :W])
        kc = conv_silu(k_ref[r, rows, :].astype(F32), ktails, r, cw[:, W:], cb[:, W:])
        kc = kc * (1.0 / math.sqrt(ML_HEAD_DIM))
        ifc = ifc_ref[r, rows, :] + bic_ref[...]
        ifr = ifr_refs[r][:, rows] + bir_ref[...]

        for h in range(ML_HEADS):
            st = r * ML_HEADS + h
            sl = slice(h * ML_HEAD_DIM, (h + 1) * ML_HEAD_DIM)
            i_col = ifc[:, h:h + 1]
            f_col = ifc[:, ML_HEADS + h:ML_HEADS + h + 1]
            i_row = ifr[h:h + 1, :]
            f_row = ifr[ML_HEADS + h:ML_HEADS + h + 1, :]
            lf_col = jnp.minimum(f_col, 0.0) - jnp.log(1.0 + jnp.exp(-jnp.abs(f_col)))
            lf_row = jnp.minimum(f_row, 0.0) - jnp.log(1.0 + jnp.exp(-jnp.abs(f_row)))
            b_col = jnp.sum(jnp.where(causal, lf_row, 0.0), axis=1, keepdims=True)
            b_row = jnp.sum(jnp.where(row <= col, lf_col, 0.0), axis=0, keepdims=True)
            b_last = jnp.sum(lf_row, axis=1, keepdims=True)
            m_st = m_cur[st]
            log_d = jnp.where(causal, b_col - b_row + i_row, -jnp.inf)
            m_inter = b_col + m_st
            m_t = jnp.maximum(m_inter, jnp.max(log_d, axis=1, keepdims=True))
            q_f = qc[:, sl]
            k_h = kc[:, sl]
            v_ext = jnp.concatenate([v_ref[r, rows, sl], ones], axis=1)
            s = _dot_nt(q_f.astype(BF16), k_h.astype(BF16)) * jnp.exp(log_d - m_t)
            w_inter = jnp.exp(m_inter - m_t)
            ct = ct_cur[st]
            num = _dot(jnp.concatenate([s.astype(BF16), (w_inter * q_f).astype(BF16)], axis=1),
                       jnp.concatenate([v_ext, ct.astype(BF16)], axis=0))
            den = num[:, ML_HEAD_DIM:]
            hh = num[:, :ML_HEAD_DIM] / jnp.maximum(jnp.abs(den), jnp.exp(-m_t))

            log_g = b_last - b_col + i_col
            m_new = jnp.maximum(b_last + m_st, jnp.max(log_g, axis=0, keepdims=True))
            decay = jnp.exp(b_last + m_st - m_new)
            wk = jnp.exp(log_g - m_new)
            ct_cur[st] = decay * ct + _dot_tn((wk * k_h).astype(BF16), v_ext)
            m_cur[st] = m_new

            hn = hh * lax.rsqrt(jnp.mean(hh * hh, axis=-1, keepdims=True) + EPS) * gout_ref[:, sl]
            stores.append((y_ref, (r, rows, sl), (_sigmoid(o_ref[r, rows, sl].astype(F32)) * hn).astype(BF16)))

    for st in range(nb * ML_HEADS):
        stores.append((ct_scr, st, ct_cur[st]))
        stores.append((m_scr, st, jnp.broadcast_to(m_cur[st], (SUBLANES, LANES))))
    for r in range(nb):
        stores.append((qtail, r, qtails[r]))
        stores.append((ktail, r, ktails[r]))
    for ref, idx, val in stores:
        ref[idx] = val


def _mlstm(proj, ifc, ifr, conv_w, conv_b, bias_col, bias_row, g_out, bsz, seq, cols, nb):
    qcol, kcol, vcol, ocol = cols
    L = ML_CHUNK * ML_CHUNKS_PER_STEP
    W = ML_HEADS * ML_HEAD_DIM
    nc = seq // L
    proj3 = proj.reshape(bsz, seq, proj.shape[1])
    ifc3 = ifc.reshape(bsz, seq, LANES)

    def colspec(cb):
        return pl.BlockSpec((nb, L, W), lambda b, c: (b, c, cb))

    def const(shape):
        return pl.BlockSpec(shape, lambda b, c: (0,) * len(shape))

    ifr_specs = [pl.BlockSpec((SUBLANES, L), lambda b, c, r=r: (0, (b * nb + r) * nc + c)) for r in range(nb)]
    y = pl.pallas_call(
        functools.partial(_mlstm_kernel, nb=nb),
        grid=(bsz // nb, nc),
        in_specs=[
            colspec(qcol), colspec(kcol), colspec(vcol), colspec(ocol),
            pl.BlockSpec((nb, L, LANES), lambda b, c: (b, c, 0)),
            *ifr_specs,
            const((ML_CONV, 2 * W)), const((1, 2 * W)),
            const((1, LANES)), const((SUBLANES, 1)), const((1, W)),
        ],
        out_specs=pl.BlockSpec((nb, L, W), lambda b, c: (b, c, 0)),
        out_shape=jax.ShapeDtypeStruct((bsz, seq, W), BF16),
        scratch_shapes=[
            pltpu.VMEM((nb, SUBLANES, W), F32),
            pltpu.VMEM((nb, SUBLANES, W), F32),
            pltpu.VMEM((nb * ML_HEADS, ML_HEAD_DIM, ML_HEAD_DIM + LANES), F32),
            pltpu.VMEM((nb * ML_HEADS, SUBLANES, LANES), F32),
        ],
        compiler_params=_cparams(("arbitrary", "arbitrary")),
        name="mlstm",
    )(proj3, proj3, proj3, proj3, ifc3, *([ifr] * nb), conv_w, conv_b, bias_col, bias_row, g_out)
    return y.reshape(bsz * seq, W)


def _memkv_kernel(mem_ref, g_ref, w_ref, gk_ref, k_ref, v_ref):
    x = mem_ref[...]
    ms = jnp.mean(x * x, axis=-1, keepdims=True)
    mn = (x * lax.rsqrt(ms + EPS) * g_ref[...]).astype(BF16)
    kv = _dot(mn, w_ref[...])
    W = XA_HEADS * XA_HEAD_DIM
    for h in range(XA_HEADS):
        sl = slice(h * XA_HEAD_DIM, (h + 1) * XA_HEAD_DIM)
        kh = kv[:, sl]
        kn = kh * lax.rsqrt(jnp.mean(kh * kh, axis=-1, keepdims=True) + EPS) * gk_ref[...]
        k_ref[:, sl] = kn.astype(BF16)
    v_ref[...] = kv[:, W:].astype(BF16)


def _memkv(mem2, g_mem, w_kv, g_k, bsz, n_mem):
    d = mem2.shape[1]
    W = XA_HEADS * XA_HEAD_DIM
    return pl.pallas_call(
        _memkv_kernel,
        grid=(bsz,),
        in_specs=[
            pl.BlockSpec((n_mem, d), lambda b: (b, 0)),
            pl.BlockSpec((1, d), lambda b: (0, 0)),
            pl.BlockSpec((d, 2 * W), lambda b: (0, 0)),
            pl.BlockSpec((1, XA_HEAD_DIM), lambda b: (0, 0)),
        ],
        out_specs=[pl.BlockSpec((n_mem, W), lambda b: (b, 0)),
                   pl.BlockSpec((n_mem, W), lambda b: (b, 0))],
        out_shape=[jax.ShapeDtypeStruct((bsz * n_mem, W), BF16)] * 2,
        compiler_params=_cparams(("arbitrary",)),
        name="memkv",
    )(mem2, g_mem, w_kv, g_k)


def _merge_kernel(x_ref, ysb_ref, yml_ref, xq_ref, kn_ref, vv_ref, gq_ref, gm_ref, wg_ref,
                  wsb_ref, wml_ref, wxa_ref, wo_ref, gf_ref, x1_ref, h2_ref):
    scale = 1.0 / math.sqrt(XA_HEAD_DIM)
    heads = []
    for hd in range(XA_HEADS):
        sl = slice(hd * XA_HEAD_DIM, (hd + 1) * XA_HEAD_DIM)
        qh = xq_ref[:, sl].astype(F32)
        qn = qh * lax.rsqrt(jnp.mean(qh * qh, axis=-1, keepdims=True) + EPS) * gq_ref[...]
        s = _dot_nt(qn.astype(BF16), kn_ref[:, sl]) * scale
        p = jnp.exp(s - jnp.max(s, axis=-1, keepdims=True))
        p = p / jnp.sum(p, axis=-1, keepdims=True)
        heads.append(_dot(p.astype(BF16), vv_ref[:, sl]).astype(BF16))
    yxa = jnp.concatenate(heads, axis=1)

    x = x_ref[...]
    d = x.shape[1]
    h = (x * lax.rsqrt(jnp.mean(x * x, axis=-1, keepdims=True) + EPS) * gm_ref[...]).astype(BF16)
    merged = None
    for br, (y, w_ref) in enumerate(((ysb_ref[...], wsb_ref), (yml_ref[...], wml_ref), (yxa, wxa_ref))):
        gate = _sigmoid(_dot_nt(h, wg_ref[br * d:(br + 1) * d, :]))
        term = gate * _dot(y, w_ref[...])
        merged = term if merged is None else merged + term
    x1 = x + _dot(merged.astype(BF16), wo_ref[...])
    x1_ref[...] = x1
    ms = jnp.mean(x1 * x1, axis=-1, keepdims=True)
    h2_ref[...] = (x1 * lax.rsqrt(ms + EPS) * gf_ref[...]).astype(BF16)


def _merge(x2, ysb, yml, proj, xq_col, kn, vv, g_q, g_mix, w_gate, w_sb, w_ml, w_xa, w_o, g_ffn, seq, tm):
    t, d = x2.shape
    wb = ysb.shape[1]
    n_mem = kn.shape[0] // (t // seq)
    tiles_per_seq = seq // tm

    def rows(width, cb=0):
        return pl.BlockSpec((tm, width), lambda i: (i, cb))

    def const(shape):
        return pl.BlockSpec(shape, lambda i: (0, 0), pipeline_mode=pl.Buffered(1))

    def per_batch(shape):
        return pl.BlockSpec(shape, lambda i: (i // tiles_per_seq, 0))

    return pl.pallas_call(
        _merge_kernel,
        grid=(t // tm,),
        in_specs=[
            rows(d), rows(wb), rows(wb), rows(wb, xq_col),
            per_batch((n_mem, wb)), per_batch((n_mem, wb)), const((1, XA_HEAD_DIM)),
            const((1, d)), const((3 * d, d)),
            const((wb, d)), const((wb, d)), const((wb, d)), const((d, d)), const((1, d)),
        ],
        out_specs=[rows(d), rows(d)],
        out_shape=[jax.ShapeDtypeStruct((t, d), F32), jax.ShapeDtypeStruct((t, d), BF16)],
        compiler_params=_cparams(("arbitrary",)),
        name="merge",
    )(x2, ysb, yml, proj, kn, vv, g_q, g_mix, w_gate, w_sb, w_ml, w_xa, w_o, g_ffn)


def _ffn_kernel(x1_ref, h2_ref, wup_ref, cw_ref, cb_ref, wd_ref, o_ref, tail_scr, act_scr,
                *, tm, tf, tiles_per_seq):
    i = pl.program_id(0)
    first = (i % tiles_per_seq) == 0
    th = tm // FFN_ROW_SPLIT

    for hh in range(FFN_ROW_SPLIT):
        rows = slice(hh * th, (hh + 1) * th)
        h2 = h2_ref[rows, :]
        for co in range(0, tf, LANES):
            cols = slice(2 * co, 2 * co + 2 * LANES)
            u = _dot(h2, wup_ref[:, cols])
            tail = tail_scr[:, cols]
            if hh == 0:
                tail = jnp.where(first, jnp.zeros_like(tail), tail)
            tail_scr[:, cols] = u[th - SUBLANES:th, :]
            head = jnp.concatenate([tail, u[0:SUBLANES, :]], axis=0)
            cw = cw_ref[:, cols]
            y = cb_ref[:, cols] + cw[FF_CONV - 1:FF_CONV, :] * u
            for back in range(1, FF_CONV):
                prev = jnp.concatenate(
                    [pltpu.roll(head, back, axis=0)[SUBLANES:, :], pltpu.roll(u, back, axis=0)[SUBLANES:, :]],
                    axis=0)
                y = y + cw[FF_CONV - 1 - back:FF_CONV - back, :] * prev
            uv, ug = y[:, :LANES], y[:, LANES:]
            act_scr[rows, co:co + LANES] = (ug * _sigmoid(ug) * uv).astype(BF16)
        o_ref[rows, :] = x1_ref[rows, :] + _dot(act_scr[rows, :], wd_ref[...])


def _interleave_val_gate(a):
    dff = a.shape[-1] // 2
    blocks = []
    for c in range(dff // LANES):
        blocks.append(a[..., c * LANES:(c + 1) * LANES])
        blocks.append(a[..., dff + c * LANES:dff + (c + 1) * LANES])
    return jnp.concatenate(blocks, axis=-1)


def _ffn(x1, h2, w_up, conv_w, conv_b, w_down, seq, tm):
    t, d = x1.shape
    dff = w_down.shape[0]

    def resident(shape):
        return pl.BlockSpec(shape, lambda i: (0, 0), pipeline_mode=pl.Buffered(1))

    return pl.pallas_call(
        functools.partial(_ffn_kernel, tm=tm, tf=dff, tiles_per_seq=seq // tm),
        grid=(t // tm,),
        in_specs=[
            pl.BlockSpec((tm, d), lambda i: (i, 0)),
            pl.BlockSpec((tm, d), lambda i: (i, 0)),
            resident((d, 2 * dff)),
            resident((FF_CONV, 2 * dff)),
            resident((1, 2 * dff)),
            resident((dff, d)),
        ],
        out_specs=pl.BlockSpec((tm, d), lambda i: (i, 0)),
        out_shape=jax.ShapeDtypeStruct((t, d), F32),
        scratch_shapes=[
            pltpu.VMEM((SUBLANES, 2 * dff), F32),
            pltpu.VMEM((tm, dff), BF16),
        ],
        compiler_params=_cparams(("arbitrary",)),
        name="ffn",
    )(x1, h2, _interleave_val_gate(w_up), _interleave_val_gate(conv_w), _interleave_val_gate(conv_b), w_down)


def kernel(x, mem, g_mix, w_in, ml_conv_w, ml_conv_b, ml_b_i, ml_b_f, ml_g_out, g_mem,
           w_mem_kv, xa_g_q, xa_g_k, w_sb_out, w_ml_out, w_xa_out, w_o, g_ffn, w_up,
           ff_conv_w, ff_conv_b, w_down):
    bsz, seq, d = x.shape
    n_mem = mem.shape[1]
    depth = w_in.shape[0]
    sbw = SB_HEADS * SB_HEAD_DIM
    mlw = ML_HEADS * ML_HEAD_DIM
    xaw = XA_HEADS * XA_HEAD_DIM
    n_wide = 3 * sbw + 4 * mlw
    if_lo, if_hi = n_wide, n_wide + 2 * ML_HEADS
    sb_scale = math.log2(math.e) / math.sqrt(SB_HEAD_DIM)

    x2 = x.reshape(bsz * seq, d)
    mem2 = mem.reshape(bsz * n_mem, d)
    for l in range(depth):
        wt = w_in[l].T
        w_main = jnp.concatenate(
            [wt[:sbw] * (-sb_scale), wt[sbw:n_wide], wt[if_hi:if_hi + xaw]], axis=0).astype(BF16)
        w_gate = wt[if_hi + xaw:].astype(BF16)
        w_ifc = jnp.pad(wt[if_lo:if_hi], ((0, LANES - 2 * ML_HEADS), (0, 0))).astype(BF16)
        proj, ifc, ifr = _inproj(x2, g_mix[l][None, :], w_main, w_ifc, tm=512, tn=4096)

        ysb = _sb_attention(proj, bsz, seq, qcol=0, kcol=sbw // LANES, vcol=2 * sbw // LANES, tq=512)

        bias = jnp.concatenate([ml_b_i[l], ml_b_f[l]])
        bias_col = jnp.pad(bias, (0, LANES - 2 * ML_HEADS))[None, :]
        bias_row = bias[:, None]
        ml0 = 3 * sbw // mlw
        yml = _mlstm(proj, ifc, ifr, ml_conv_w[l], ml_conv_b[l][None, :], bias_col, bias_row,
                     ml_g_out[l].reshape(1, mlw), bsz, seq, cols=(ml0, ml0 + 1, ml0 + 2, ml0 + 3), nb=ML_BATCH)

        kn, vv = _memkv(mem2, g_mem[l][None, :], w_mem_kv[l].astype(BF16), xa_g_k[l][None, :], bsz, n_mem)
        x1, h2 = _merge(x2, ysb, yml, proj, n_wide // xaw, kn, vv, xa_g_q[l][None, :],
                        g_mix[l][None, :], w_gate,
                        w_sb_out[l].astype(BF16), w_ml_out[l].astype(BF16), w_xa_out[l].astype(BF16),
                        w_o[l].astype(BF16), g_ffn[l][None, :], seq, tm=1024)

        x2 = _ffn(x1, h2, w_up[l].astype(BF16), ff_conv_w[l], ff_conv_b[l][None, :],
                  w_down[l].astype(BF16), seq, tm=512)
    return x2.reshape(bsz, seq, d)
```
